```python
import math
import jax, jax.numpy as jnp
from jax import lax
import numpy as np

D_MODEL = 2048
BATCH = 2
SEQ = 4096
DEPTH = 2
DEC_BATCH = 8
DEC_SEQ = 1
PAST_LEN = 16384
PAGE_SIZE = 128

DIFF_HEAD_DIM = 128
DIFF_HEADS = D_MODEL // (2 * DIFF_HEAD_DIM)
DIFF_V_DIM = 2 * DIFF_HEAD_DIM
N_DIFF_LAYERS = (DEPTH + 1) // 2
MOBA_HEAD_DIM = 128
MOBA_HEADS = D_MODEL // MOBA_HEAD_DIM
MOBA_BLOCK = 256
MOBA_TOPK = 3
D_FF = -(-8 * D_MODEL // (3 * 256)) * 256
DIFF_QBLOCK = 128
MOBA_QCHUNK = 16
NORM_EPS = 1e-6
NEG_INF = -1e30

kernel_name = "diff_moba_hybrid_decode_step"


def rms_norm(x, g):
    xf = x.astype(jnp.float32)
    y = xf * lax.rsqrt(jnp.mean(xf * xf, axis=-1, keepdims=True) + NORM_EPS)
    return (y * g.astype(jnp.float32)).astype(x.dtype)


def swiglu(x, w_gate_up, w_down):
    g, u = jnp.split(x @ w_gate_up, 2, axis=-1)
    return (jax.nn.silu(g) * u) @ w_down


def alibi_slopes(n_heads):
    return jnp.asarray([2.0 ** (-8.0 * (h + 1) / n_heads) for h in range(n_heads)], jnp.float32)


def lambda_init_value(layer_idx):
    return 0.8 - 0.6 * math.exp(-0.3 * layer_idx)


def gather_pages(cache, page_table):
    b, n = page_table.shape
    rows = cache[page_table]
    return rows.reshape((b, n * cache.shape[1]) + cache.shape[2:])


def diff_lambda_value(lam_params, lam_init):
    p = lam_params.astype(jnp.float32)
    return jnp.exp(jnp.sum(p[0] * p[1])) - jnp.exp(jnp.sum(p[2] * p[3])) + lam_init


def diff_core(q, qpos, segments, lam, slopes):
    scale = DIFF_HEAD_DIM ** -0.5
    scores = []
    for k, _, kpos in segments:
        dist = (qpos[:, None] - kpos[None, :]).astype(jnp.float32)
        s = jnp.einsum("bqhcd,bkhcd->bhcqk", q, k).astype(jnp.float32) * scale
        s = s - slopes[:, None, None, None] * dist
        scores.append(jnp.where(dist >= 0, s, NEG_INF))
    p = jax.nn.softmax(jnp.concatenate(scores, axis=-1), axis=-1)
    w = p[:, :, 0] - lam * p[:, :, 1]
    out, start = None, 0
    for _, v, kpos in segments:
        n = kpos.shape[0]
        part = jnp.einsum("bhqk,bkhd->bqhd", w[..., start:start + n].astype(v.dtype), v)
        out = part if out is None else out + part
        start += n
    return out


def diff_prompt(q, k, v, lam, slopes):
    B, S = q.shape[:2]
    nb = S // DIFF_QBLOCK
    pos = jnp.arange(S, dtype=jnp.int32)
    qb = jnp.moveaxis(q.reshape((B, nb, DIFF_QBLOCK) + q.shape[2:]), 1, 0)
    pb = pos.reshape(nb, DIFF_QBLOCK)
    o = lax.map(lambda a: diff_core(a[0], a[1], [(k, v, pos)], lam, slopes), (qb, pb))
    return jnp.moveaxis(o, 0, 1).reshape((B, S) + o.shape[3:])


def diff_finish(o, subln, lam_init):
    o = rms_norm(o, subln) * (1.0 - lam_init)
    return o.reshape(o.shape[:2] + (-1,))


def moba_blocks(k_parts, v_parts):
    B, _, H, D = k_parts[0].shape
    L = sum(p.shape[1] for p in k_parts)
    nb = -(-L // MOBA_BLOCK)
    pad = nb * MOBA_BLOCK - L
    zk = jnp.zeros((B, pad, H, D), k_parts[0].dtype)
    zv = jnp.zeros((B, pad, H, D), v_parts[0].dtype)
    kb = jnp.concatenate(list(k_parts) + [zk], axis=1).reshape(B, nb, MOBA_BLOCK, H, D)
    vb = jnp.concatenate(list(v_parts) + [zv], axis=1).reshape(B, nb, MOBA_BLOCK, H, D)
    kmean = jnp.mean(kb.astype(jnp.float32), axis=2)
    return kb, vb, kmean


def moba_core(q, qpos, kb, vb, kmean, slopes):
    B, Q, H, D = q.shape
    nb = kb.shape[1]
    own = qpos // MOBA_BLOCK
    gate = jnp.einsum("bqhd,bnhd->bqhn", q.astype(jnp.float32), kmean)
    past = jnp.arange(nb, dtype=jnp.int32)[None, :] < own[:, None]
    gate = jnp.where(past[None, :, None, :], gate, NEG_INF)
    _, top = lax.top_k(gate, min(MOBA_TOPK, nb))
    own_b = jnp.broadcast_to(own[None, :, None, None], (B, Q, H, 1)).astype(top.dtype)
    sel = jnp.concatenate([top, own_b], axis=-1)
    ok = jnp.concatenate([top < own_b, jnp.ones((B, Q, H, 1), bool)], axis=-1)
    bi = jnp.arange(B)[:, None, None, None]
    hi = jnp.arange(H)[None, None, :, None]
    kg = kb[bi, sel, :, hi, :]
    vg = vb[bi, sel, :, hi, :]
    kpos = sel[..., None] * MOBA_BLOCK + jnp.arange(MOBA_BLOCK, dtype=sel.dtype)
    dist = qpos[None, :, None, None, None] - kpos
    s = jnp.einsum("bqhd,bqhjtd->bqhjt", q, kg).astype(jnp.float32) * (MOBA_HEAD_DIM ** -0.5)
    s = s - slopes[None, None, :, None, None] * dist.astype(jnp.float32)
    s = jnp.where(ok[..., None] & (dist >= 0), s, NEG_INF)
    p = jax.nn.softmax(s.reshape(B, Q, H, -1), axis=-1).reshape(s.shape)
    return jnp.einsum("bqhjt,bqhjtd->bqhd", p.astype(vg.dtype), vg)


def moba_prompt(q, kb, vb, kmean, slopes):
    B, S, H, D = q.shape
    nc = S // MOBA_QCHUNK
    qc = jnp.moveaxis(q.reshape(B, nc, MOBA_QCHUNK, H, D), 1, 0)
    pc = jnp.arange(S, dtype=jnp.int32).reshape(nc, MOBA_QCHUNK)
    o = lax.map(lambda a: moba_core(a[0], a[1], kb, vb, kmean, slopes), (qc, pc))
    return jnp.moveaxis(o, 0, 1).reshape(B, S, H, D)


def setup_inputs(seed: int = 0) -> dict:
    key = jax.random.key(seed)
    k = jax.random.split(key, 17)
    f32 = jnp.float32
    n_pages = PAST_LEN // PAGE_SIZE
    n_used = DEC_BATCH * n_pages
    n_pool = n_used + n_used // 4

    def nrm(kk, shape, s=1.0):
        return jax.random.normal(kk, shape, f32) * s

    page_table = jax.random.permutation(k[6], n_pool)[:n_used].reshape(DEC_BATCH, n_pages).astype(jnp.int32)
    return {
        "x_prompt": nrm(k[0], (BATCH, SEQ, D_MODEL)),
        "x_sample": nrm(k[1], (DEC_BATCH, DEC_SEQ, D_MODEL)),
        "cache_k_l0": nrm(k[2], (n_pool, PAGE_SIZE, DIFF_HEADS, 2 * DIFF_HEAD_DIM)),
        "cache_v_l0": nrm(k[3], (n_pool, PAGE_SIZE, DIFF_HEADS, DIFF_V_DIM)),
        "cache_k_l1": nrm(k[4], (n_pool, PAGE_SIZE, MOBA_HEADS, MOBA_HEAD_DIM)),
        "cache_v_l1": nrm(k[5], (n_pool, PAGE_SIZE, MOBA_HEADS, MOBA_HEAD_DIM)),
        "page_table": page_table,
        "w_qkv": nrm(k[7], (DEPTH, D_MODEL, 3 * D_MODEL), D_MODEL ** -0.5),
        "w_o": nrm(k[8], (DEPTH, D_MODEL, D_MODEL), D_MODEL ** -0.5),
        "w_gate_up": nrm(k[9], (DEPTH, D_MODEL, 2 * D_FF), D_MODEL ** -0.5),
        "w_down": nrm(k[10], (DEPTH, D_FF, D_MODEL), D_FF ** -0.5),
        "g_mix_pre": 1.0 + nrm(k[11], (DEPTH, D_MODEL), 0.02),
        "g_mix_post": 1.0 + nrm(k[12], (DEPTH, D_MODEL), 0.02),
        "g_ffn_pre": 1.0 + nrm(k[13], (DEPTH, D_MODEL), 0.02),
        "g_ffn_post": 1.0 + nrm(k[14], (DEPTH, D_MODEL), 0.02),
        "diff_lambda": nrm(k[15], (N_DIFF_LAYERS, 4, DIFF_HEAD_DIM), 0.1),
        "diff_subln": 1.0 + nrm(k[16], (N_DIFF_LAYERS, DIFF_V_DIM), 0.02),
    }


def reference(x_prompt, x_sample, cache_k_l0, cache_v_l0, cache_k_l1, cache_v_l1, page_table,
              w_qkv, w_o, w_gate_up, w_down, g_mix_pre, g_mix_post, g_ffn_pre, g_ffn_post,
              diff_lambda, diff_subln):
    caches = [(cache_k_l0, cache_v_l0), (cache_k_l1, cache_v_l1)]
    bp, sp = x_prompt.shape[:2]
    bs, ds = x_sample.shape[:2]
    past_len = page_table.shape[1] * cache_k_l0.shape[1]
    pos_past = jnp.arange(past_len, dtype=jnp.int32)
    pos_s = past_len + jnp.arange(ds, dtype=jnp.int32)

    hp, hs = x_prompt, x_sample
    rows = []
    for i in range(DEPTH):
        ck, cv = caches[i]
        xp = rms_norm(hp, g_mix_pre[i])
        xs = rms_norm(hs, g_mix_pre[i])
        qp, kp, vp = jnp.split(xp @ w_qkv[i], 3, axis=-1)
        qs, ks, vs = jnp.split(xs @ w_qkv[i], 3, axis=-1)
        if i % 2 == 0:
            j = i // 2
            lam_init = lambda_init_value(i)
            lam = diff_lambda_value(diff_lambda[j], lam_init)
            slopes = alibi_slopes(DIFF_HEADS)
            qk_p = (bp, sp, DIFF_HEADS, 2, DIFF_HEAD_DIM)
            qk_s = (bs, ds, DIFF_HEADS, 2, DIFF_HEAD_DIM)
            qp, kp = qp.reshape(qk_p), kp.reshape(qk_p)
            vp = vp.reshape(bp, sp, DIFF_HEADS, DIFF_V_DIM)
            qs, ks = qs.reshape(qk_s), ks.reshape(qk_s)
            vs = vs.reshape(bs, ds, DIFF_HEADS, DIFF_V_DIM)
            op = diff_prompt(qp, kp, vp, lam, slopes)
            k_past = gather_pages(ck, page_table).reshape(bs, past_len, DIFF_HEADS, 2, DIFF_HEAD_DIM)
            v_past = gather_pages(cv, page_table)
            os_ = diff_core(qs, pos_s, [(k_past, v_past, pos_past), (ks, vs, pos_s)], lam, slopes)
            op = diff_finish(op, diff_subln[j], lam_init)
            os_ = diff_finish(os_, diff_subln[j], lam_init)
            rows.append((kp.reshape(bp, sp, DIFF_HEADS, 2 * DIFF_HEAD_DIM), vp,
                         ks.reshape(bs, ds, DIFF_HEADS, 2 * DIFF_HEAD_DIM), vs))
        else:
            slopes = alibi_slopes(MOBA_HEADS)
            hp_shape = (bp, sp, MOBA_HEADS, MOBA_HEAD_DIM)
            hs_shape = (bs, ds, MOBA_HEADS, MOBA_HEAD_DIM)
            qp, kp, vp = qp.reshape(hp_shape), kp.reshape(hp_shape), vp.reshape(hp_shape)
            qs, ks, vs = qs.reshape(hs_shape), ks.reshape(hs_shape), vs.reshape(hs_shape)
            kb, vb, kmean = moba_blocks([kp], [vp])
            op = moba_prompt(qp, kb, vb, kmean, slopes).reshape(bp, sp, D_MODEL)
            k_past = gather_pages(ck, page_table)
            v_past = gather_pages(cv, page_table)
            kb_s, vb_s, kmean_s = moba_blocks([k_past, ks], [v_past, vs])
            os_ = moba_core(qs, pos_s, kb_s, vb_s, kmean_s, slopes).reshape(bs, ds, D_MODEL)
            rows.append((kp, vp, ks, vs))
        hp = hp + rms_norm(op @ w_o[i], g_mix_post[i])
        hs = hs + rms_norm(os_ @ w_o[i], g_mix_post[i])
        hp = hp + rms_norm(swiglu(rms_norm(hp, g_ffn_pre[i]), w_gate_up[i], w_down[i]), g_ffn_post[i])
        hs = hs + rms_norm(swiglu(rms_norm(hs, g_ffn_pre[i]), w_gate_up[i], w_down[i]), g_ffn_post[i])

    (k0p, v0p, k0s, v0s), (k1p, v1p, k1s, v1s) = rows
    return (hp, hs, k0p, v0p, k0s, v0s, k1p, v1p, k1s, v1s)
```

```python
import functools
import math

import jax
import jax.numpy as jnp
from jax import lax
from jax.experimental import pallas as pl
from jax.experimental.pallas import tpu as pltpu

F32 = jnp.float32
BF16 = jnp.bfloat16

NORM_EPS = 1e-6
NEG_INF = -1e30
DIFF_HEAD_DIM = 128
MOBA_HEAD_DIM = 128
MOBA_BLOCK = 256
MOBA_TOPK = 3

LANES = 128
SUBLANES = 8
VMEM_LIMIT = 56 * 1024 * 1024


def _cparams(sem):
    return pltpu.CompilerParams(dimension_semantics=sem, vmem_limit_bytes=VMEM_LIMIT)


def _nt_dot(a, b):
    return lax.dot_general(a, b, (((1,), (1,)), ((), ())), preferred_element_type=F32)


def _rms_scale(x):
    return lax.rsqrt(jnp.mean(x * x, axis=-1, keepdims=True) + NORM_EPS)


def _norm_qkv_kernel(x_ref, g_ref, wq_ref, wk_ref, wv_ref,
                     q_ref, k_ref, v_ref, kb_ref, vb_ref, xn_ref):
    @pl.when(pl.program_id(1) == 0)
    def _():
        x = x_ref[...]
        xn_ref[...] = ((x * _rms_scale(x)) * g_ref[...]).astype(BF16)

    xn = xn_ref[...]
    q = jnp.dot(xn, wq_ref[...], preferred_element_type=F32)
    k = jnp.dot(xn, wk_ref[...], preferred_element_type=F32)
    v = jnp.dot(xn, wv_ref[...], preferred_element_type=F32)
    q_ref[...] = q.astype(q_ref.dtype)
    k_ref[...] = k
    v_ref[...] = v
    kb_ref[...] = k.astype(BF16)
    vb_ref[...] = v.astype(BF16)


def _norm_qkv(x, g, w, *, tm, tn, q_dtype):
    m, d = x.shape
    nj = d // tn
    wspec = lambda off: pl.BlockSpec((d, tn), lambda i, j: (0, j + off))
    ospec = pl.BlockSpec((tm, tn), lambda i, j: (i, j))
    return pl.pallas_call(
        _norm_qkv_kernel,
        grid=(m // tm, nj),
        in_specs=[pl.BlockSpec((tm, d), lambda i, j: (i, 0)),
                  pl.BlockSpec((1, d), lambda i, j: (0, 0)),
                  wspec(0), wspec(nj), wspec(2 * nj)],
        out_specs=[ospec] * 5,
        out_shape=[jax.ShapeDtypeStruct((m, d), q_dtype),
                   jax.ShapeDtypeStruct((m, d), F32),
                   jax.ShapeDtypeStruct((m, d), F32),
                   jax.ShapeDtypeStruct((m, d), BF16),
                   jax.ShapeDtypeStruct((m, d), BF16)],
        scratch_shapes=[pltpu.VMEM((tm, d), BF16)],
        compiler_params=_cparams(("parallel", "arbitrary")),
        name="norm_qkv",
    )(x, g, w, w, w)


def _norm_swiglu_kernel(x_ref, g_ref, wg_ref, wu_ref, h_ref, xn_ref):
    @pl.when(pl.program_id(1) == 0)
    def _():
        x = x_ref[...]
        xn_ref[...] = ((x * _rms_scale(x)) * g_ref[...]).astype(BF16)

    xn = xn_ref[...]
    gate = jnp.dot(xn, wg_ref[...], preferred_element_type=F32)
    up = jnp.dot(xn, wu_ref[...], preferred_element_type=F32)
    h_ref[...] = ((gate * (1.0 / (1.0 + jnp.exp(-gate)))) * up).astype(BF16)


def _norm_swiglu(x, g, w, *, tm, tn):
    m, d = x.shape
    f = w.shape[1] // 2
    nj = f // tn
    return pl.pallas_call(
        _norm_swiglu_kernel,
        grid=(m // tm, nj),
        in_specs=[pl.BlockSpec((tm, d), lambda i, j: (i, 0)),
                  pl.BlockSpec((1, d), lambda i, j: (0, 0)),
                  pl.BlockSpec((d, tn), lambda i, j: (0, j)),
                  pl.BlockSpec((d, tn), lambda i, j: (0, j + nj))],
        out_specs=pl.BlockSpec((tm, tn), lambda i, j: (i, j)),
        out_shape=jax.ShapeDtypeStruct((m, f), BF16),
        scratch_shapes=[pltpu.VMEM((tm, d), BF16)],
        compiler_params=_cparams(("parallel", "arbitrary")),
        name="norm_swiglu",
    )(x, g, w, w)


def _mm_norm_res_kernel(a_ref, w_ref, g_ref, r_ref, o_ref, acc_ref):
    kk = pl.program_id(1)

    @pl.when(kk == 0)
    def _():
        acc_ref[...] = jnp.zeros_like(acc_ref)

    acc_ref[...] += jnp.dot(a_ref[...].astype(BF16), w_ref[...],
                            preferred_element_type=F32)

    @pl.when(kk == pl.num_programs(1) - 1)
    def _():
        y = acc_ref[...]
        o_ref[...] = r_ref[...] + (y * _rms_scale(y)) * g_ref[...]


def _mm_norm_res(a, w, g, res, *, tm, tk):
    m, kdim = a.shape
    n = w.shape[1]
    return pl.pallas_call(
        _mm_norm_res_kernel,
        grid=(m // tm, kdim // tk),
        in_specs=[pl.BlockSpec((tm, tk), lambda i, k: (i, k)),
                  pl.BlockSpec((tk, n), lambda i, k: (k, 0)),
                  pl.BlockSpec((1, n), lambda i, k: (0, 0)),
                  pl.BlockSpec((tm, n), lambda i, k: (i, 0))],
        out_specs=pl.BlockSpec((tm, n), lambda i, k: (i, 0)),
        out_shape=jax.ShapeDtypeStruct((m, n), F32),
        scratch_shapes=[pltpu.VMEM((tm, n), F32)],
        compiler_params=_cparams(("parallel", "arbitrary")),
        name="mm_norm_res",
    )(a, w, g, res)


def _diff_lambda(lp, lam_init):
    a = jnp.sum(lp[0:1] * lp[1:2], axis=-1, keepdims=True)
    b = jnp.sum(lp[2:3] * lp[3:4], axis=-1, keepdims=True)
    return jnp.exp(a) - jnp.exp(b) + lam_init


def _diff_prompt_kernel(slope_ref, lam_ref, subln_ref, q_ref, k_ref, v_ref, o_ref,
                        m_ref, l_ref, acc_ref, *, tile, lam_init):
    i = pl.program_id(2)
    dk = DIFF_HEAD_DIM
    scale = dk ** -0.5
    slope = slope_ref[0][0:1, 0:1]
    q = q_ref[...]
    rows = lax.broadcasted_iota(jnp.int32, (tile, tile), 0)
    cols = lax.broadcasted_iota(jnp.int32, (tile, tile), 1)
    rel = (rows - cols).astype(F32)
    bias0 = -slope * rel

    m_ref[...] = jnp.full_like(m_ref, NEG_INF)
    l_ref[...] = jnp.zeros_like(l_ref)
    acc_ref[...] = jnp.zeros_like(acc_ref)

    def update(t, masked):
        start = pl.multiple_of(t * tile, tile)
        k = k_ref[pl.ds(start, tile), :]
        v = v_ref[pl.ds(start, tile), :]
        bias = bias0 - slope * ((i - t) * tile).astype(F32)
        for c in range(2):
            s = _nt_dot(q[:, c * dk:(c + 1) * dk], k[:, c * dk:(c + 1) * dk]) * scale + bias
            if masked:
                s = jnp.where(rel >= 0, s, NEG_INF)
            m_prev = m_ref[c]
            m_new = jnp.maximum(m_prev, jnp.max(s, axis=-1, keepdims=True))
            alpha = jnp.exp(m_prev - m_new)
            p = jnp.exp(s - m_new)
            l_ref[c] = alpha * l_ref[c] + jnp.sum(p, axis=-1, keepdims=True)
            acc_ref[c] = alpha * acc_ref[c] + jnp.dot(p.astype(BF16), v,
                                                      preferred_element_type=F32)
            m_ref[c] = m_new

    def body(t, carry):
        update(t, False)
        return carry

    lax.fori_loop(0, i, body, 0)
    update(i, True)

    lam = _diff_lambda(lam_ref[...], lam_init)
    o = acc_ref[0] / l_ref[0] - lam * (acc_ref[1] / l_ref[1])
    o = ((o * _rms_scale(o)) * subln_ref[...]) * (1.0 - lam_init)
    o_ref[...] = o.astype(o_ref.dtype)


def _diff_prompt(q, k, v, slopes, lam_params, subln, *, batch, seq, heads, lam_init, tile):
    hw = 2 * DIFF_HEAD_DIM
    nq = seq // tile
    kern = functools.partial(_diff_prompt_kernel, tile=tile, lam_init=lam_init)
    return pl.pallas_call(
        kern,
        grid=(batch, heads, nq),
        in_specs=[pl.BlockSpec((1, SUBLANES, LANES), lambda b, h, i: (h, 0, 0)),
                  pl.BlockSpec((4, DIFF_HEAD_DIM), lambda b, h, i: (0, 0)),
                  pl.BlockSpec((1, hw), lambda b, h, i: (0, 0)),
                  pl.BlockSpec((tile, hw), lambda b, h, i: (b * nq + i, h)),
                  pl.BlockSpec((seq, hw), lambda b, h, i: (b, h)),
                  pl.BlockSpec((seq, hw), lambda b, h, i: (b, h))],
        out_specs=pl.BlockSpec((tile, hw), lambda b, h, i: (b * nq + i, h)),
        out_shape=jax.ShapeDtypeStruct(q.shape, BF16),
        scratch_shapes=[pltpu.VMEM((2, tile, 1), F32),
                        pltpu.VMEM((2, tile, 1), F32),
                        pltpu.VMEM((2, tile, hw), F32)],
        compiler_params=_cparams(("parallel", "parallel", "arbitrary")),
        name="diff_prompt",
    )(slopes, lam_params, subln, q, k, v)


def _group_ones(width):
    r = lax.broadcasted_iota(jnp.int32, (width, width), 0) // LANES
    c = lax.broadcasted_iota(jnp.int32, (width, width), 1) // LANES
    return (r == c).astype(BF16)


def _lane_group_sums(x, ones):
    shape = x.shape
    x2 = x.reshape((-1, shape[-1])).astype(BF16)
    return jnp.dot(x2, ones, preferred_element_type=F32).reshape(shape)


def _twice(x):
    return jnp.concatenate([x, x], axis=-1)


def _diff_decode_kernel(pt_ref, slope_ref, lam_ref, subln_ref, q_ref, ks_ref, vs_ref, *rest,
                        pages_per_step, page, past_len, lam_init):
    npg = pages_per_step
    k_refs = rest[:npg]
    v_refs = rest[npg:2 * npg]
    o_ref = rest[2 * npg]
    m_ref, l_ref, acc_ref = rest[2 * npg + 1:]
    j = pl.program_id(1)
    dk = DIFF_HEAD_DIM
    scale = dk ** -0.5
    q = q_ref[0]
    slope = slope_ref[...]
    ones = _group_ones(2 * dk)

    @pl.when(j == 0)
    def _():
        m_ref[...] = jnp.full_like(m_ref, NEG_INF)
        l_ref[...] = jnp.zeros_like(l_ref)
        acc_ref[...] = jnp.zeros_like(acc_ref)

    def update(s, v):
        m_prev = m_ref[...]
        m_new = jnp.maximum(m_prev, jnp.max(s, axis=0))
        alpha = jnp.exp(m_prev - m_new)
        p = jnp.exp(s - m_new[None])
        l_ref[...] = alpha * l_ref[...] + jnp.sum(p, axis=0)
        for c in range(2):
            pc = _twice(p[:, :, c * dk:(c + 1) * dk])
            ac = _twice(alpha[:, c * dk:(c + 1) * dk])
            acc_ref[c] = ac * acc_ref[c] + jnp.sum(pc * v, axis=0)
        m_ref[...] = m_new

    t_idx = lax.broadcasted_iota(jnp.int32, (page, SUBLANES, 2 * dk), 0)
    for pg in range(npg):
        k = k_refs[pg][0]
        v = v_refs[pg][0]
        kpos = (j * npg + pg) * page + t_idx
        dist = (past_len - kpos).astype(F32)
        s = _lane_group_sums(k * q[None], ones) * scale - slope[None] * dist
        update(s, v)

    @pl.when(j == pl.num_programs(1) - 1)
    def _():
        s_self = _lane_group_sums(ks_ref[0] * q, ones) * scale
        update(s_self[None], vs_ref[0][None])
        lam = _diff_lambda(lam_ref[...], lam_init)
        l = l_ref[...]
        o = (acc_ref[0] / _twice(l[:, :dk])) - lam * (acc_ref[1] / _twice(l[:, dk:]))
        o = ((o * _rms_scale(o)) * subln_ref[...]) * (1.0 - lam_init)
        o_ref[0] = o


def _diff_decode(q, ks, vs, cache_k, cache_v, page_table, slopes, lam_params, subln, *,
                 lam_init, pages_per_step):
    b, h, w = q.shape
    page = cache_k.shape[1]
    n_pages = page_table.shape[1]
    npg = pages_per_step
    kern = functools.partial(_diff_decode_kernel, pages_per_step=npg, page=page,
                             past_len=n_pages * page, lam_init=lam_init)

    def cache_spec(pg):
        return pl.BlockSpec((1, page, h, w),
                            lambda bi, j, pt: (pt[bi, j * npg + pg], 0, 0, 0))

    vec = pl.BlockSpec((1, h, w), lambda bi, j, pt: (bi, 0, 0))
    grid_spec = pltpu.PrefetchScalarGridSpec(
        num_scalar_prefetch=1,
        grid=(b, n_pages // npg),
        in_specs=[pl.BlockSpec((h, w), lambda bi, j, pt: (0, 0)),
                  pl.BlockSpec((4, DIFF_HEAD_DIM), lambda bi, j, pt: (0, 0)),
                  pl.BlockSpec((1, w), lambda bi, j, pt: (0, 0)),
                  vec, vec, vec]
                 + [cache_spec(pg) for pg in range(npg)]
                 + [cache_spec(pg) for pg in range(npg)],
        out_specs=vec,
        scratch_shapes=[pltpu.VMEM((h, w), F32),
                        pltpu.VMEM((h, w), F32),
                        pltpu.VMEM((2, h, w), F32)],
    )
    return pl.pallas_call(
        kern,
        grid_spec=grid_spec,
        out_shape=jax.ShapeDtypeStruct((b, h, w), F32),
        compiler_params=_cparams(("parallel", "arbitrary")),
        name="diff_decode",
    )(page_table, slopes, lam_params, subln, q, ks, vs,
      *([cache_k] * npg), *([cache_v] * npg))


def _moba_kmean_kernel(k_ref, o_ref, *, nblk):
    k = k_ref[...]
    km = jnp.mean(k.reshape(nblk, MOBA_BLOCK, k.shape[-1]), axis=1)
    o_ref[0, 0] = jnp.zeros(o_ref.shape[2:], F32)
    o_ref[0, 0, 0:nblk, :] = km


def _moba_kmean(k, *, batch, seq, heads):
    d = MOBA_HEAD_DIM
    nblk = seq // MOBA_BLOCK
    return pl.pallas_call(
        functools.partial(_moba_kmean_kernel, nblk=nblk),
        grid=(batch, heads),
        in_specs=[pl.BlockSpec((seq, d), lambda b, h: (b, h))],
        out_specs=pl.BlockSpec((1, 1, LANES, d), lambda b, h: (b, h, 0, 0)),
        out_shape=jax.ShapeDtypeStruct((batch, heads, LANES, d), F32),
        compiler_params=_cparams(("parallel", "parallel")),
        name="moba_kmean",
    )(k)


def _top_blocks(gate, n_valid, lane):
    valid = lane < n_valid
    g = jnp.where(valid, gate, NEG_INF)
    chosen = jnp.zeros(gate.shape, F32)
    for _ in range(MOBA_TOPK):
        mx = jnp.max(g, axis=-1, keepdims=True)
        first = jnp.min(jnp.where(g == mx, lane, float(LANES)), axis=-1, keepdims=True)
        pick = lane == first
        chosen = jnp.where(pick & valid, 1.0, chosen)
        g = jnp.where(pick, -jnp.inf, g)
    return chosen


def _moba_prompt_kernel(slope_ref, km_ref, q_ref, k_ref, v_ref, o_ref,
                        m_ref, l_ref, acc_ref):
    i = pl.program_id(2)
    d = MOBA_HEAD_DIM
    blk = MOBA_BLOCK
    scale = d ** -0.5
    slope = slope_ref[0][0:1, 0:1]
    q = q_ref[...]
    rows = lax.broadcasted_iota(jnp.int32, (blk, blk), 0)
    cols = lax.broadcasted_iota(jnp.int32, (blk, blk), 1)
    rel = (rows - cols).astype(F32)
    bias0 = -slope * rel

    km = km_ref[0, 0]
    km_hi = km.astype(BF16)
    km_lo = (km - km_hi.astype(F32)).astype(BF16)
    gate = _nt_dot(q, km_hi) + _nt_dot(q, km_lo)
    lane = lax.broadcasted_iota(jnp.int32, (blk, LANES), 1).astype(F32)
    chosen = _top_blocks(gate, i.astype(F32), lane)
    gate_bias = jnp.where(chosen > 0.5, 0.0, NEG_INF).astype(BF16)
    q_aug = jnp.concatenate([q, gate_bias], axis=-1)

    k_own = k_ref[pl.ds(pl.multiple_of(i * blk, blk), blk), :]
    v_own = v_ref[pl.ds(pl.multiple_of(i * blk, blk), blk), :]
    s = jnp.where(rel >= 0, _nt_dot(q, k_own) * scale + bias0, NEG_INF)
    m0 = jnp.max(s, axis=-1, keepdims=True)
    p = jnp.exp(s - m0)
    m_ref[...] = m0
    l_ref[...] = jnp.sum(p, axis=-1, keepdims=True)
    acc_ref[...] = jnp.dot(p.astype(BF16), v_own, preferred_element_type=F32)

    def body(n, carry):
        start = pl.multiple_of(n * blk, blk)
        k = k_ref[pl.ds(start, blk), :]
        v = v_ref[pl.ds(start, blk), :]
        onehot = (lax.broadcasted_iota(jnp.int32, (blk, LANES), 1) == n).astype(BF16)
        k_aug = jnp.concatenate([k, onehot], axis=-1)
        bias = bias0 - slope * ((i - n) * blk).astype(F32)
        s = _nt_dot(q_aug, k_aug) * scale + bias
        m_prev = m_ref[...]
        m_new = jnp.maximum(m_prev, jnp.max(s, axis=-1, keepdims=True))
        alpha = jnp.exp(m_prev - m_new)
        p = jnp.exp(s - m_new)
        l_ref[...] = alpha * l_ref[...] + jnp.sum(p, axis=-1, keepdims=True)
        acc_ref[...] = alpha * acc_ref[...] + jnp.dot(p.astype(BF16), v,
                                                      preferred_element_type=F32)
        m_ref[...] = m_new
        return carry

    lax.fori_loop(0, i, body, 0)
    o_ref[...] = (acc_ref[...] / l_ref[...]).astype(o_ref.dtype)


def _moba_prompt(q, k, v, kmean, slopes, *, batch, seq, heads):
    d = MOBA_HEAD_DIM
    blk = MOBA_BLOCK
    nq = seq // blk
    return pl.pallas_call(
        _moba_prompt_kernel,
        grid=(batch, heads, nq),
        in_specs=[pl.BlockSpec((1, SUBLANES, LANES), lambda b, h, i: (h, 0, 0)),
                  pl.BlockSpec((1, 1, LANES, d), lambda b, h, i: (b, h, 0, 0)),
                  pl.BlockSpec((blk, d), lambda b, h, i: (b * nq + i, h)),
                  pl.BlockSpec((seq, d), lambda b, h, i: (b, h)),
                  pl.BlockSpec((seq, d), lambda b, h, i: (b, h))],
        out_specs=pl.BlockSpec((blk, d), lambda b, h, i: (b * nq + i, h)),
        out_shape=jax.ShapeDtypeStruct(q.shape, BF16),
        scratch_shapes=[pltpu.VMEM((blk, 1), F32),
                        pltpu.VMEM((blk, 1), F32),
                        pltpu.VMEM((blk, d), F32)],
        compiler_params=_cparams(("parallel", "parallel", "arbitrary")),
        name="moba_prompt",
    )(slopes, kmean, q, k, v)


def _moba_select_kernel(pt_ref, q_ref, ka_ref, kb_ref, sel_ref, gate_ref, *, n_blocks):
    n = pl.program_id(1)
    ksum = jnp.sum(ka_ref[0], axis=0) + jnp.sum(kb_ref[0], axis=0)
    g = jnp.sum(q_ref[0] * (ksum * (1.0 / MOBA_BLOCK)), axis=-1, keepdims=True)
    lane = lax.broadcasted_iota(jnp.int32, gate_ref.shape, 1)

    @pl.when(n == 0)
    def _():
        gate_ref[...] = jnp.full_like(gate_ref, NEG_INF)

    gate_ref[...] = jnp.where(lane == n, g, gate_ref[...])

    @pl.when(n == n_blocks - 1)
    def _():
        lane_f = lane.astype(F32)
        cand = jnp.where(lane < n_blocks, gate_ref[...], NEG_INF)
        sel = jnp.zeros(cand.shape, F32)
        for r in range(MOBA_TOPK):
            mx = jnp.max(cand, axis=-1, keepdims=True)
            first = jnp.min(jnp.where(cand == mx, lane_f, float(LANES)), axis=-1, keepdims=True)
            sel = jnp.where(lane == r, first, sel)
            cand = jnp.where(lane_f == first, -jnp.inf, cand)
        sel_ref[0] = sel.astype(jnp.int32)


def _moba_select(q, cache_k, page_table):
    b, h, d = q.shape
    page = cache_k.shape[1]
    ppb = MOBA_BLOCK // page
    assert ppb == 2
    n_blocks = page_table.shape[1] // ppb

    def kspec(half):
        return pl.BlockSpec((1, page, h, d), lambda bi, n, pt: (pt[bi, n * ppb + half], 0, 0, 0))

    grid_spec = pltpu.PrefetchScalarGridSpec(
        num_scalar_prefetch=1,
        grid=(b, n_blocks),
        in_specs=[pl.BlockSpec((1, h, d), lambda bi, n, pt: (bi, 0, 0)), kspec(0), kspec(1)],
        out_specs=pl.BlockSpec((1, h, LANES), lambda bi, n, pt: (bi, 0, 0)),
        scratch_shapes=[pltpu.VMEM((h, LANES), F32)],
    )
    return pl.pallas_call(
        functools.partial(_moba_select_kernel, n_blocks=n_blocks),
        grid_spec=grid_spec,
        out_shape=jax.ShapeDtypeStruct((b, h, LANES), jnp.int32),
        compiler_params=_cparams(("parallel", "arbitrary")),
        name="moba_select",
    )(page_table, q, cache_k, cache_k)


def _moba_decode_kernel(pg_ref, pos_ref, slope_ref, q_ref, ks_ref, vs_ref, k_ref, v_ref, o_ref,
                        m_ref, l_ref, acc_ref, *, page, past_len, n_steps):
    bi = pl.program_id(0)
    h = pl.program_id(1)
    r = pl.program_id(2)
    hl = h % SUBLANES
    d = MOBA_HEAD_DIM
    scale = d ** -0.5
    q = q_ref[0]
    slope = slope_ref[...]
    ones = _group_ones(d)

    @pl.when((hl == 0) & (r == 0))
    def _():
        m_ref[...] = jnp.full_like(m_ref, NEG_INF)
        l_ref[...] = jnp.zeros_like(l_ref)
        acc_ref[...] = jnp.zeros_like(acc_ref)

    def update(s, v, live):
        m_prev = m_ref[...]
        m_new = jnp.where(live, jnp.maximum(m_prev, jnp.max(s, axis=0)), m_prev)
        alpha = jnp.exp(m_prev - m_new)
        p = jnp.where(live[None], jnp.exp(s - m_new[None]), 0.0)
        l_ref[...] = alpha * l_ref[...] + jnp.sum(p, axis=0)
        acc_ref[...] = alpha * acc_ref[...] + jnp.sum(p * v, axis=0)
        m_ref[...] = m_new

    k = k_ref[0]
    v = v_ref[0]
    t_idx = lax.broadcasted_iota(jnp.int32, (page, SUBLANES, d), 0)
    kpos = pos_ref[bi, h * n_steps + r] + t_idx
    dist = (past_len - kpos).astype(F32)
    s = _lane_group_sums(k * q[None], ones) * scale - slope[None] * dist
    row = lax.broadcasted_iota(jnp.int32, (SUBLANES, d), 0)
    update(s, v, row == hl)

    @pl.when((hl == SUBLANES - 1) & (r == n_steps - 1))
    def _():
        s_self = _lane_group_sums(ks_ref[0] * q, ones) * scale
        update(s_self[None], vs_ref[0][None], row >= 0)
        o_ref[0] = acc_ref[...] / l_ref[...]


def _moba_decode(q, ks, vs, cache_k, cache_v, phys_pages, first_pos, slopes, *, past_len):
    b, h, d = q.shape
    page = cache_k.shape[1]
    n_steps = phys_pages.shape[1] // h

    def cspec():
        return pl.BlockSpec((1, page, SUBLANES, d),
                            lambda bi, hh, r, pg, pos: (pg[bi, hh * n_steps + r], 0, hh // SUBLANES, 0))

    vec = pl.BlockSpec((1, SUBLANES, d), lambda bi, hh, r, pg, pos: (bi, hh // SUBLANES, 0))
    grid_spec = pltpu.PrefetchScalarGridSpec(
        num_scalar_prefetch=2,
        grid=(b, h, n_steps),
        in_specs=[pl.BlockSpec((SUBLANES, d), lambda bi, hh, r, pg, pos: (hh // SUBLANES, 0)),
                  vec, vec, vec, cspec(), cspec()],
        out_specs=vec,
        scratch_shapes=[pltpu.VMEM((SUBLANES, d), F32),
                        pltpu.VMEM((SUBLANES, d), F32),
                        pltpu.VMEM((SUBLANES, d), F32)],
    )
    return pl.pallas_call(
        functools.partial(_moba_decode_kernel, page=page, past_len=past_len, n_steps=n_steps),
        grid_spec=grid_spec,
        out_shape=jax.ShapeDtypeStruct((b, h, d), F32),
        compiler_params=_cparams(("parallel", "arbitrary", "arbitrary")),
        name="moba_decode",
    )(phys_pages, first_pos, slopes, q, ks, vs, cache_k, cache_v)


def _alibi_slopes(n_heads):
    return jnp.asarray([2.0 ** (-8.0 * (h + 1) / n_heads) for h in range(n_heads)], F32)


def _lambda_init_value(layer_idx):
    return 0.8 - 0.6 * math.exp(-0.3 * layer_idx)


def kernel(x_prompt, x_sample, cache_k_l0, cache_v_l0, cache_k_l1, cache_v_l1, page_table,
           w_qkv, w_o, w_gate_up, w_down, g_mix_pre, g_mix_post, g_ffn_pre, g_ffn_post,
           diff_lambda, diff_subln):
    bp, sp, dm = x_prompt.shape
    bs, ds, _ = x_sample.shape
    assert ds == 1
    depth = w_qkv.shape[0]
    page = cache_k_l0.shape[1]
    past_len = page_table.shape[1] * page
    diff_heads = cache_k_l0.shape[2]
    moba_heads = cache_k_l1.shape[2]
    dff = w_down.shape[1]

    hp = x_prompt.reshape(bp * sp, dm)
    hs = x_sample.reshape(bs * ds, dm)
    wqkv_b = w_qkv.astype(BF16)
    wo_b = w_o.astype(BF16)
    wgu_b = w_gate_up.astype(BF16)
    wdn_b = w_down.astype(BF16)

    rows = []
    for i in range(depth):
        g_pre = g_mix_pre[i].reshape(1, dm)
        qp, kp, vp, kpb, vpb = _norm_qkv(hp, g_pre, wqkv_b[i], tm=512, tn=512, q_dtype=BF16)
        qs, ks, vs, _, _ = _norm_qkv(hs, g_pre, wqkv_b[i], tm=bs * ds, tn=512, q_dtype=F32)
        if i % 2 == 0:
            jd = i // 2
            lam_init = _lambda_init_value(i)
            hw = 2 * DIFF_HEAD_DIM
            sl = _alibi_slopes(diff_heads)
            slopes_p = jnp.broadcast_to(sl[:, None, None], (diff_heads, SUBLANES, LANES))
            slopes_s = jnp.broadcast_to(sl[:, None], (diff_heads, hw))
            subln = diff_subln[jd].reshape(1, hw)
            op = _diff_prompt(qp, kpb, vpb, slopes_p, diff_lambda[jd], subln,
                              batch=bp, seq=sp, heads=diff_heads, lam_init=lam_init, tile=256)
            os_ = _diff_decode(qs.reshape(bs, diff_heads, hw), ks.reshape(bs, diff_heads, hw),
                               vs.reshape(bs, diff_heads, hw), cache_k_l0, cache_v_l0,
                               page_table, slopes_s, diff_lambda[jd], subln,
                               lam_init=lam_init, pages_per_step=4)
            os_ = os_.reshape(bs * ds, dm)
            rows.append((kp.reshape(bp, sp, diff_heads, hw), vp.reshape(bp, sp, diff_heads, hw),
                         ks.reshape(bs, ds, diff_heads, hw), vs.reshape(bs, ds, diff_heads, hw)))
        else:
            d = MOBA_HEAD_DIM
            sl = _alibi_slopes(moba_heads)
            slopes_p = jnp.broadcast_to(sl[:, None, None], (moba_heads, SUBLANES, LANES))
            slopes_s = jnp.broadcast_to(sl[:, None], (moba_heads, d))
            kmean = _moba_kmean(kp, batch=bp, seq=sp, heads=moba_heads)
            op = _moba_prompt(qp, kpb, vpb, kmean, slopes_p, batch=bp, seq=sp, heads=moba_heads)
            q3 = qs.reshape(bs, moba_heads, d)
            k3 = ks.reshape(bs, moba_heads, d)
            v3 = vs.reshape(bs, moba_heads, d)
            sel = _moba_select(q3, cache_k_l1, page_table)[:, :, :MOBA_TOPK]
            ppb = MOBA_BLOCK // page
            logical = (sel[..., None] * ppb + jnp.arange(ppb, dtype=jnp.int32)).reshape(bs, -1)
            phys = jnp.take_along_axis(page_table, logical, axis=1)
            os_ = _moba_decode(q3, k3, v3, cache_k_l1, cache_v_l1, phys, logical * page,
                                    slopes_s, past_len=past_len)
            os_ = os_.reshape(bs * ds, dm)
            rows.append((kp.reshape(bp, sp, moba_heads, d), vp.reshape(bp, sp, moba_heads, d),
                         ks.reshape(bs, ds, moba_heads, d), vs.reshape(bs, ds, moba_heads, d)))

        g_post = g_mix_post[i].reshape(1, dm)
        hp = _mm_norm_res(op, wo_b[i], g_post, hp, tm=512, tk=dm)
        hs = _mm_norm_res(os_, wo_b[i], g_post, hs, tm=bs * ds, tk=dm)
        g_fpre = g_ffn_pre[i].reshape(1, dm)
        g_fpost = g_ffn_post[i].reshape(1, dm)
        fp = _norm_swiglu(hp, g_fpre, wgu_b[i], tm=1024, tn=512)
        hp = _mm_norm_res(fp, wdn_b[i], g_fpost, hp, tm=1024, tk=512)
        fs = _norm_swiglu(hs, g_fpre, wgu_b[i], tm=bs * ds, tn=512)
        hs = _mm_norm_res(fs, wdn_b[i], g_fpost, hs, tm=bs * ds, tk=512)

    (k0p, v0p, k0s, v0s), (k1p, v1p, k1s, v1s) = rows
    return (hp.reshape(bp, sp, dm), hs.reshape(bs, ds, dm),
            k0p, v0p, k0s, v0s, k1p, v1p, k1s, v1s)
```

```python
import functools
import math

import jax
import jax.numpy as jnp
from jax import lax
from jax.experimental import pallas as pl
from jax.experimental.pallas import tpu as pltpu

F32 = jnp.float32
BF16 = jnp.bfloat16

NORM_EPS = 1e-6
NEG_INF = -1e30
LOG2E = math.log2(math.e)
DIFF_HEAD_DIM = 128
MOBA_HEAD_DIM = 128
MOBA_BLOCK = 256
MOBA_TOPK = 3
ATTN_TILE = MOBA_BLOCK
KV_GROUP = 4

LANES = 128
SUBLANES = 8
VMEM_LIMIT = 56 * 1024 * 1024


def _cparams(sem):
    return pltpu.CompilerParams(dimension_semantics=sem, vmem_limit_bytes=VMEM_LIMIT)


def _nt_dot(a, b):
    return lax.dot_general(a, b, (((1,), (1,)), ((), ())), preferred_element_type=F32)


def _rms_scale(x):
    return lax.rsqrt(jnp.mean(x * x, axis=-1, keepdims=True) + NORM_EPS)


def _norm_qkv_kernel(x_ref, g_ref, wq_ref, wk_ref, wv_ref, q_ref, k_ref, v_ref, *rest, tile):
    xn_ref = rest[-1]

    @pl.when(pl.program_id(1) == 0)
    def _():
        x = x_ref[...]
        xn_ref[...] = ((x * _rms_scale(x)) * g_ref[...]).astype(BF16)

    xn = xn_ref[...]
    q = jnp.dot(xn, wq_ref[...], preferred_element_type=F32)
    k = jnp.dot(xn, wk_ref[...], preferred_element_type=F32)
    v = jnp.dot(xn, wv_ref[...], preferred_element_type=F32)
    q_ref[...] = q.astype(q_ref.dtype)
    k_ref[...] = k
    v_ref[...] = v
    if len(rest) == 3:
        kb_ref, vt_ref = rest[:2]
        kb_ref[...] = k.astype(BF16)
        for u in range(vt_ref.shape[0]):
            vt_ref[u] = v[u * tile:(u + 1) * tile, :].T.astype(BF16)


def _norm_qkv(x, g, w, layer, *, tm, tn, q_dtype, tile=None):
    m, d = x.shape
    nj = d // tn
    wspec = lambda off: pl.BlockSpec((None, d, tn), lambda i, j: (layer, 0, j + off))
    ospec = pl.BlockSpec((tm, tn), lambda i, j: (i, j))
    out_specs = [ospec] * 3
    out_shape = [jax.ShapeDtypeStruct((m, d), q_dtype),
                 jax.ShapeDtypeStruct((m, d), F32),
                 jax.ShapeDtypeStruct((m, d), F32)]
    if tile is not None:
        out_specs += [ospec, pl.BlockSpec((tm // tile, tn, tile), lambda i, j: (i, j, 0))]
        out_shape += [jax.ShapeDtypeStruct((m, d), BF16),
                      jax.ShapeDtypeStruct((m // tile, d, tile), BF16)]
    return pl.pallas_call(
        functools.partial(_norm_qkv_kernel, tile=tile),
        grid=(m // tm, nj),
        in_specs=[pl.BlockSpec((tm, d), lambda i, j: (i, 0)),
                  pl.BlockSpec((1, d), lambda i, j: (0, 0)),
                  wspec(0), wspec(nj), wspec(2 * nj)],
        out_specs=out_specs,
        out_shape=out_shape,
        scratch_shapes=[pltpu.VMEM((tm, d), BF16)],
        compiler_params=_cparams(("parallel", "arbitrary")),
        name="norm_qkv",
    )(x, g, w, w, w)


def _norm_swiglu_kernel(x_ref, g_ref, wg_ref, wu_ref, h_ref, xn_ref):
    @pl.when(pl.program_id(1) == 0)
    def _():
        x = x_ref[...]
        xn_ref[...] = ((x * _rms_scale(x)) * g_ref[...]).astype(BF16)

    xn = xn_ref[...]
    gate = jnp.dot(xn, wg_ref[...], preferred_element_type=F32)
    up = jnp.dot(xn, wu_ref[...], preferred_element_type=F32)
    h_ref[...] = ((gate * (1.0 / (1.0 + jnp.exp(-gate)))) * up).astype(BF16)


def _norm_swiglu(x, g, w, layer, *, tm, tn):
    m, d = x.shape
    f = w.shape[2] // 2
    nj = f // tn
    return pl.pallas_call(
        _norm_swiglu_kernel,
        grid=(m // tm, nj),
        in_specs=[pl.BlockSpec((tm, d), lambda i, j: (i, 0)),
                  pl.BlockSpec((1, d), lambda i, j: (0, 0)),
                  pl.BlockSpec((None, d, tn), lambda i, j: (layer, 0, j)),
                  pl.BlockSpec((None, d, tn), lambda i, j: (layer, 0, j + nj))],
        out_specs=pl.BlockSpec((tm, tn), lambda i, j: (i, j)),
        out_shape=jax.ShapeDtypeStruct((m, f), BF16),
        scratch_shapes=[pltpu.VMEM((tm, d), BF16)],
        compiler_params=_cparams(("parallel", "arbitrary")),
        name="norm_swiglu",
    )(x, g, w, w)


def _mm_norm_res_kernel(a_ref, w_ref, g_ref, r_ref, o_ref, acc_ref):
    kk = pl.program_id(1)

    @pl.when(kk == 0)
    def _():
        acc_ref[...] = jnp.zeros_like(acc_ref)

    acc_ref[...] += jnp.dot(a_ref[...].astype(BF16), w_ref[...],
                            preferred_element_type=F32)

    @pl.when(kk == pl.num_programs(1) - 1)
    def _():
        y = acc_ref[...]
        o_ref[...] = r_ref[...] + (y * _rms_scale(y)) * g_ref[...]


def _mm_norm_res(a, w, layer, g, res, *, tm, tk):
    m, kdim = a.shape
    n = w.shape[2]
    return pl.pallas_call(
        _mm_norm_res_kernel,
        grid=(m // tm, kdim // tk),
        in_specs=[pl.BlockSpec((tm, tk), lambda i, k: (i, k)),
                  pl.BlockSpec((None, tk, n), lambda i, k: (layer, k, 0)),
                  pl.BlockSpec((1, n), lambda i, k: (0, 0)),
                  pl.BlockSpec((tm, n), lambda i, k: (i, 0))],
        out_specs=pl.BlockSpec((tm, n), lambda i, k: (i, 0)),
        out_shape=jax.ShapeDtypeStruct((m, n), F32),
        scratch_shapes=[pltpu.VMEM((tm, n), F32)],
        compiler_params=_cparams(("parallel", "arbitrary")),
        name="mm_norm_res",
    )(a, w, g, res)


def _diff_lambda(lp, lam_init):
    a = jnp.sum(lp[0:1] * lp[1:2], axis=-1, keepdims=True)
    b = jnp.sum(lp[2:3] * lp[3:4], axis=-1, keepdims=True)
    return jnp.exp(a) - jnp.exp(b) + lam_init


def _diff_prompt_kernel(slope_ref, lam_ref, subln_ref, q_ref, k_ref, vt_ref, o_ref, acc_ref,
                        *, tile, lam_init):
    i = pl.program_id(2)
    dk = DIFF_HEAD_DIM
    c2 = dk ** -0.5 * LOG2E
    slope2 = slope_ref[0][0:1, 0:1] * LOG2E
    q = q_ref[...]
    krow = lax.broadcasted_iota(jnp.int32, (tile, tile), 0)
    qcol = lax.broadcasted_iota(jnp.int32, (tile, tile), 1)
    kbias = slope2 * krow.astype(F32)
    acc_ref[...] = jnp.zeros_like(acc_ref)

    def scores(t):
        k = k_ref[pl.ds(pl.multiple_of(t * tile, tile), tile), :]
        return [_nt_dot(k[:, c * dk:(c + 1) * dk], q[:, c * dk:(c + 1) * dk]) * c2 + kbias
                for c in range(2)]

    def update(tiles, sc, shifts, state):
        out_m, out_l = [], []
        for c in range(2):
            m_prev, l_prev = state[c], state[2 + c]
            m_new = m_prev
            for s, shift in zip(sc, shifts):
                m_new = jnp.maximum(m_new, jnp.max(s[c], axis=0, keepdims=True) + shift)
            alpha = jnp.exp2(m_prev - m_new)
            l_new = alpha * l_prev
            pv = None
            for t, s, shift in zip(tiles, sc, shifts):
                p = jnp.exp2(s[c] - (m_new - shift))
                l_new = l_new + jnp.sum(p, axis=0, keepdims=True)
                part = jnp.dot(vt_ref[t], p.astype(BF16), preferred_element_type=F32)
                pv = part if pv is None else pv + part
            acc_ref[c] = alpha * acc_ref[c] + pv
            out_m.append(m_new)
            out_l.append(l_new)
        return tuple(out_m + out_l)

    def body(g, state):
        tiles = [g * KV_GROUP + u for u in range(KV_GROUP)]
        shifts = [jnp.where(t < i, -slope2 * ((i - t) * tile).astype(F32), NEG_INF) for t in tiles]
        return update(tiles, [scores(t) for t in tiles], shifts, state)

    neg = jnp.full((1, tile), NEG_INF, F32)
    zero = jnp.zeros((1, tile), F32)
    n_groups = (i + KV_GROUP - 1) // KV_GROUP
    state = lax.fori_loop(0, n_groups, body, (neg, neg, zero, zero))
    diag = [jnp.where(qcol >= krow, s, NEG_INF) for s in scores(i)]
    _, _, l0, l1 = update([i], [diag], [jnp.zeros((1, 1), F32)], state)

    lam = _diff_lambda(lam_ref[...], lam_init)
    o = acc_ref[0] * (1.0 / l0) - lam * (acc_ref[1] * (1.0 / l1))
    inv_rms = lax.rsqrt(jnp.mean(o * o, axis=0, keepdims=True) + NORM_EPS)
    o = ((o * inv_rms) * subln_ref[...]) * (1.0 - lam_init)
    o_ref[...] = o.T.astype(o_ref.dtype)


def _diff_prompt(q, k, vt, slopes, lam_params, subln_col, *, batch, seq, heads, lam_init, tile):
    hw = 2 * DIFF_HEAD_DIM
    nq = seq // tile
    kern = functools.partial(_diff_prompt_kernel, tile=tile, lam_init=lam_init)
    return pl.pallas_call(
        kern,
        grid=(batch, heads, nq),
        in_specs=[pl.BlockSpec((1, SUBLANES, LANES), lambda b, h, i: (h, 0, 0)),
                  pl.BlockSpec((4, DIFF_HEAD_DIM), lambda b, h, i: (0, 0)),
                  pl.BlockSpec((hw, tile), lambda b, h, i: (0, 0)),
                  pl.BlockSpec((tile, hw), lambda b, h, i: (b * nq + i, h)),
                  pl.BlockSpec((seq, hw), lambda b, h, i: (b, h)),
                  pl.BlockSpec((nq, hw, tile), lambda b, h, i: (b, h, 0))],
        out_specs=pl.BlockSpec((tile, hw), lambda b, h, i: (b * nq + i, h)),
        out_shape=jax.ShapeDtypeStruct(q.shape, BF16),
        scratch_shapes=[pltpu.VMEM((2, hw, tile), F32)],
        compiler_params=_cparams(("parallel", "parallel", "arbitrary")),
        name="diff_prompt",
    )(slopes, lam_params, subln_col, q, k, vt)


def _group_ones(width):
    r = lax.broadcasted_iota(jnp.int32, (width, width), 0) // LANES
    c = lax.broadcasted_iota(jnp.int32, (width, width), 1) // LANES
    return (r == c).astype(BF16)


def _lane_group_sums(x, ones):
    shape = x.shape
    x2 = x.reshape((-1, shape[-1])).astype(BF16)
    return jnp.dot(x2, ones, preferred_element_type=F32).reshape(shape)


def _twice(x):
    return jnp.concatenate([x, x], axis=-1)


def _diff_decode_kernel(pt_ref, slope_ref, lam_ref, subln_ref, q_ref, ks_ref, vs_ref, *rest,
                        pages_per_step, page, past_len, lam_init):
    npg = pages_per_step
    k_refs = rest[:npg]
    v_refs = rest[npg:2 * npg]
    o_ref = rest[2 * npg]
    m_ref, l_ref, acc_ref = rest[2 * npg + 1:]
    j = pl.program_id(1)
    dk = DIFF_HEAD_DIM
    scale = dk ** -0.5
    q = q_ref[0]
    slope = slope_ref[...]
    ones = _group_ones(2 * dk)

    @pl.when(j == 0)
    def _():
        m_ref[...] = jnp.full_like(m_ref, NEG_INF)
        l_ref[...] = jnp.zeros_like(l_ref)
        acc_ref[...] = jnp.zeros_like(acc_ref)

    def update(s, v):
        m_prev = m_ref[...]
        m_new = jnp.maximum(m_prev, jnp.max(s, axis=0))
        alpha = jnp.exp(m_prev - m_new)
        p = jnp.exp(s - m_new[None])
        l_ref[...] = alpha * l_ref[...] + jnp.sum(p, axis=0)
        for c in range(2):
            pc = _twice(p[:, :, c * dk:(c + 1) * dk])
            ac = _twice(alpha[:, c * dk:(c + 1) * dk])
            acc_ref[c] = ac * acc_ref[c] + jnp.sum(pc * v, axis=0)
        m_ref[...] = m_new

    t_idx = lax.broadcasted_iota(jnp.int32, (page, SUBLANES, 2 * dk), 0)
    for pg in range(npg):
        k = k_refs[pg][0]
        v = v_refs[pg][0]
        kpos = (j * npg + pg) * page + t_idx
        dist = (past_len - kpos).astype(F32)
        s = _lane_group_sums(k * q[None], ones) * scale - slope[None] * dist
        update(s, v)

    @pl.when(j == pl.num_programs(1) - 1)
    def _():
        s_self = _lane_group_sums(ks_ref[0] * q, ones) * scale
        update(s_self[None], vs_ref[0][None])
        lam = _diff_lambda(lam_ref[...], lam_init)
        l = l_ref[...]
        o = (acc_ref[0] / _twice(l[:, :dk])) - lam * (acc_ref[1] / _twice(l[:, dk:]))
        o = ((o * _rms_scale(o)) * subln_ref[...]) * (1.0 - lam_init)
        o_ref[0] = o


def _diff_decode(q, ks, vs, cache_k, cache_v, page_table, slopes, lam_params, subln, *,
                 lam_init, pages_per_step):
    b, h, w = q.shape
    page = cache_k.shape[1]
    n_pages = page_table.shape[1]
    npg = pages_per_step
    kern = functools.partial(_diff_decode_kernel, pages_per_step=npg, page=page,
                             past_len=n_pages * page, lam_init=lam_init)

    def cache_spec(pg):
        return pl.BlockSpec((1, page, h, w),
                            lambda bi, j, pt: (pt[bi, j * npg + pg], 0, 0, 0))

    vec = pl.BlockSpec((1, h, w), lambda bi, j, pt: (bi, 0, 0))
    grid_spec = pltpu.PrefetchScalarGridSpec(
        num_scalar_prefetch=1,
        grid=(b, n_pages // npg),
        in_specs=[pl.BlockSpec((h, w), lambda bi, j, pt: (0, 0)),
                  pl.BlockSpec((4, DIFF_HEAD_DIM), lambda bi, j, pt: (0, 0)),
                  pl.BlockSpec((1, w), lambda bi, j, pt: (0, 0)),
                  vec, vec, vec]
                 + [cache_spec(pg) for pg in range(npg)]
                 + [cache_spec(pg) for pg in range(npg)],
        out_specs=vec,
        scratch_shapes=[pltpu.VMEM((h, w), F32),
                        pltpu.VMEM((h, w), F32),
                        pltpu.VMEM((2, h, w), F32)],
    )
    return pl.pallas_call(
        kern,
        grid_spec=grid_spec,
        out_shape=jax.ShapeDtypeStruct((b, h, w), F32),
        compiler_params=_cparams(("parallel", "arbitrary")),
        name="diff_decode",
    )(page_table, slopes, lam_params, subln, q, ks, vs,
      *([cache_k] * npg), *([cache_v] * npg))


def _moba_kmean_kernel(k_ref, o_ref, *, nblk):
    k = k_ref[...]
    o_ref[0, 0] = jnp.mean(k.reshape(nblk, MOBA_BLOCK, k.shape[-1]), axis=1)


def _moba_kmean(k, *, batch, seq, heads):
    d = MOBA_HEAD_DIM
    nblk = seq // MOBA_BLOCK
    return pl.pallas_call(
        functools.partial(_moba_kmean_kernel, nblk=nblk),
        grid=(batch, heads),
        in_specs=[pl.BlockSpec((seq, d), lambda b, h: (b, h))],
        out_specs=pl.BlockSpec((1, 1, nblk, d), lambda b, h: (b, h, 0, 0)),
        out_shape=jax.ShapeDtypeStruct((batch, heads, nblk, d), F32),
        compiler_params=_cparams(("parallel", "parallel")),
        name="moba_kmean",
    )(k)


def _top_blocks(gate, n_valid, idx, axis):
    valid = idx < n_valid
    g = jnp.where(valid, gate, NEG_INF)
    chosen = jnp.zeros(gate.shape, F32)
    past_end = float(gate.shape[axis])
    for _ in range(MOBA_TOPK):
        mx = jnp.max(g, axis=axis, keepdims=True)
        first = jnp.min(jnp.where(g == mx, idx, past_end), axis=axis, keepdims=True)
        pick = idx == first
        chosen = jnp.where(pick & valid, 1.0, chosen)
        g = jnp.where(pick, -jnp.inf, g)
    return chosen


def _moba_prompt_kernel(slope_ref, km_ref, q_ref, k_ref, vt_ref, o_ref, acc_ref, gb_ref):
    i = pl.program_id(2)
    d = MOBA_HEAD_DIM
    blk = MOBA_BLOCK
    c2 = d ** -0.5 * LOG2E
    slope2 = slope_ref[0][0:1, 0:1] * LOG2E
    q = q_ref[...]
    krow = lax.broadcasted_iota(jnp.int32, (blk, blk), 0)
    qcol = lax.broadcasted_iota(jnp.int32, (blk, blk), 1)
    kbias = slope2 * krow.astype(F32)

    km = km_ref[0, 0]
    km_hi = km.astype(BF16)
    km_lo = (km - km_hi.astype(F32)).astype(BF16)
    gate = _nt_dot(km_hi, q) + _nt_dot(km_lo, q)
    nrow = lax.broadcasted_iota(jnp.int32, gate.shape, 0).astype(F32)
    chosen = _top_blocks(gate, i.astype(F32), nrow, 0)
    gb_ref[...] = jnp.where(chosen > 0.5, 0.0, NEG_INF)

    def scores(n):
        k = k_ref[pl.ds(pl.multiple_of(n * blk, blk), blk), :]
        return _nt_dot(k, q) * c2 + kbias

    s = jnp.where(qcol >= krow, scores(i), NEG_INF)
    m0 = jnp.max(s, axis=0, keepdims=True)
    p = jnp.exp2(s - m0)
    l0 = jnp.sum(p, axis=0, keepdims=True)
    acc_ref[...] = jnp.dot(vt_ref[i], p.astype(BF16), preferred_element_type=F32)

    def body(g, state):
        m_prev, l_prev = state
        blocks = [g * KV_GROUP + u for u in range(KV_GROUP)]
        sc = [scores(n) for n in blocks]
        shifts = [gb_ref[pl.ds(n, 1), :] - slope2 * ((i - n) * blk).astype(F32) for n in blocks]
        m_new = m_prev
        for s, shift in zip(sc, shifts):
            m_new = jnp.maximum(m_new, jnp.max(s, axis=0, keepdims=True) + shift)
        alpha = jnp.exp2(m_prev - m_new)
        l_new = alpha * l_prev
        pv = None
        for n, s, shift in zip(blocks, sc, shifts):
            p = jnp.exp2(s - (m_new - shift))
            l_new = l_new + jnp.sum(p, axis=0, keepdims=True)
            part = jnp.dot(vt_ref[n], p.astype(BF16), preferred_element_type=F32)
            pv = part if pv is None else pv + part
        acc_ref[...] = alpha * acc_ref[...] + pv
        return m_new, l_new

    n_groups = (i + KV_GROUP - 1) // KV_GROUP
    _, l_fin = lax.fori_loop(0, n_groups, body, (m0, l0))
    o_ref[...] = (acc_ref[...] * (1.0 / l_fin)).T.astype(o_ref.dtype)


def _moba_prompt(q, k, vt, kmean, slopes, *, batch, seq, heads):
    d = MOBA_HEAD_DIM
    blk = MOBA_BLOCK
    nq = seq // blk
    return pl.pallas_call(
        _moba_prompt_kernel,
        grid=(batch, heads, nq),
        in_specs=[pl.BlockSpec((1, SUBLANES, LANES), lambda b, h, i: (h, 0, 0)),
                  pl.BlockSpec((1, 1, nq, d), lambda b, h, i: (b, h, 0, 0)),
                  pl.BlockSpec((blk, d), lambda b, h, i: (b * nq + i, h)),
                  pl.BlockSpec((seq, d), lambda b, h, i: (b, h)),
                  pl.BlockSpec((nq, d, blk), lambda b, h, i: (b, h, 0))],
        out_specs=pl.BlockSpec((blk, d), lambda b, h, i: (b * nq + i, h)),
        out_shape=jax.ShapeDtypeStruct(q.shape, BF16),
        scratch_shapes=[pltpu.VMEM((d, blk), F32),
                        pltpu.VMEM((nq, blk), F32)],
        compiler_params=_cparams(("parallel", "parallel", "arbitrary")),
        name="moba_prompt",
    )(slopes, kmean, q, k, vt)


def _moba_select_kernel(pt_ref, q_ref, *rest, n_blocks, blocks_per_step, pages_per_block):
    n_in = blocks_per_step * pages_per_block
    k_refs = rest[:n_in]
    sel_ref, gate_ref = rest[n_in:]
    step = pl.program_id(1)
    lane = lax.broadcasted_iota(jnp.int32, gate_ref.shape, 1)

    @pl.when(step == 0)
    def _():
        gate_ref[...] = jnp.full_like(gate_ref, NEG_INF)

    gates = gate_ref[...]
    for u in range(blocks_per_step):
        ksum = jnp.sum(k_refs[u * pages_per_block][0], axis=0)
        for pg in range(1, pages_per_block):
            ksum = ksum + jnp.sum(k_refs[u * pages_per_block + pg][0], axis=0)
        g = jnp.sum(q_ref[0] * (ksum * (1.0 / MOBA_BLOCK)), axis=-1, keepdims=True)
        gates = jnp.where(lane == step * blocks_per_step + u, g, gates)
    gate_ref[...] = gates

    @pl.when(step == pl.num_programs(1) - 1)
    def _():
        lane_f = lane.astype(F32)
        cand = jnp.where(lane < n_blocks, gate_ref[...], NEG_INF)
        sel = jnp.zeros(cand.shape, F32)
        for r in range(MOBA_TOPK):
            mx = jnp.max(cand, axis=-1, keepdims=True)
            first = jnp.min(jnp.where(cand == mx, lane_f, float(LANES)), axis=-1, keepdims=True)
            sel = jnp.where(lane == r, first, sel)
            cand = jnp.where(lane_f == first, -jnp.inf, cand)
        sel_ref[0] = sel.astype(jnp.int32)


def _moba_select(q, cache_k, page_table, *, blocks_per_step):
    b, h, d = q.shape
    page = cache_k.shape[1]
    ppb = MOBA_BLOCK // page
    n_blocks = page_table.shape[1] // ppb
    assert n_blocks <= LANES and n_blocks % blocks_per_step == 0
    n_in = blocks_per_step * ppb

    def kspec(u):
        return pl.BlockSpec((1, page, h, d), lambda bi, n, pt: (pt[bi, n * n_in + u], 0, 0, 0))

    grid_spec = pltpu.PrefetchScalarGridSpec(
        num_scalar_prefetch=1,
        grid=(b, n_blocks // blocks_per_step),
        in_specs=[pl.BlockSpec((1, h, d), lambda bi, n, pt: (bi, 0, 0))]
                 + [kspec(u) for u in range(n_in)],
        out_specs=pl.BlockSpec((1, h, LANES), lambda bi, n, pt: (bi, 0, 0)),
        scratch_shapes=[pltpu.VMEM((h, LANES), F32)],
    )
    return pl.pallas_call(
        functools.partial(_moba_select_kernel, n_blocks=n_blocks,
                          blocks_per_step=blocks_per_step, pages_per_block=ppb),
        grid_spec=grid_spec,
        out_shape=jax.ShapeDtypeStruct((b, h, LANES), jnp.int32),
        compiler_params=_cparams(("parallel", "arbitrary")),
        name="moba_select",
    )(page_table, q, *([cache_k] * n_in))


def _moba_decode_kernel(pg_ref, pos_ref, slope_ref, q_ref, ks_ref, vs_ref, *rest,
                        page, past_len, n_pages):
    k_refs = rest[:n_pages]
    v_refs = rest[n_pages:2 * n_pages]
    o_ref = rest[2 * n_pages]
    m_ref, l_ref, acc_ref = rest[2 * n_pages + 1:]
    bi = pl.program_id(0)
    h = pl.program_id(1)
    hl = h % SUBLANES
    d = MOBA_HEAD_DIM
    scale = d ** -0.5
    q = q_ref[0]
    slope = slope_ref[...]
    ones = _group_ones(d)

    @pl.when(hl == 0)
    def _():
        m_ref[...] = jnp.full_like(m_ref, NEG_INF)
        l_ref[...] = jnp.zeros_like(l_ref)
        acc_ref[...] = jnp.zeros_like(acc_ref)

    def update(ss, vv, live):
        m_prev = m_ref[...]
        m_all = m_prev
        for s in ss:
            m_all = jnp.maximum(m_all, jnp.max(s, axis=0))
        m_new = jnp.where(live, m_all, m_prev)
        alpha = jnp.exp(m_prev - m_new)
        l_new = alpha * l_ref[...]
        acc_new = alpha * acc_ref[...]
        for s, v in zip(ss, vv):
            p = jnp.where(live[None], jnp.exp(s - m_new[None]), 0.0)
            l_new = l_new + jnp.sum(p, axis=0)
            acc_new = acc_new + jnp.sum(p * v, axis=0)
        l_ref[...] = l_new
        acc_ref[...] = acc_new
        m_ref[...] = m_new

    t_idx = lax.broadcasted_iota(jnp.int32, (page, SUBLANES, d), 0)
    ss = []
    for r in range(n_pages):
        kpos = pos_ref[bi, h * n_pages + r] + t_idx
        dist = (past_len - kpos).astype(F32)
        ss.append(_lane_group_sums(k_refs[r][0] * q[None], ones) * scale - slope[None] * dist)
    row = lax.broadcasted_iota(jnp.int32, (SUBLANES, d), 0)
    update(ss, [v_refs[r][0] for r in range(n_pages)], row == hl)

    @pl.when(hl == SUBLANES - 1)
    def _():
        s_self = _lane_group_sums(ks_ref[0] * q, ones) * scale
        update([s_self[None]], [vs_ref[0][None]], row >= 0)
        o_ref[0] = acc_ref[...] / l_ref[...]


def _moba_decode(q, ks, vs, cache_k, cache_v, phys_pages, first_pos, slopes, *, past_len):
    b, h, d = q.shape
    page = cache_k.shape[1]
    n_pages = phys_pages.shape[1] // h

    def cspec(r):
        return pl.BlockSpec((1, page, SUBLANES, d),
                            lambda bi, hh, pg, pos: (pg[bi, hh * n_pages + r], 0, hh // SUBLANES, 0))

    vec = pl.BlockSpec((1, SUBLANES, d), lambda bi, hh, pg, pos: (bi, hh // SUBLANES, 0))
    grid_spec = pltpu.PrefetchScalarGridSpec(
        num_scalar_prefetch=2,
        grid=(b, h),
        in_specs=[pl.BlockSpec((SUBLANES, d), lambda bi, hh, pg, pos: (hh // SUBLANES, 0)),
                  vec, vec, vec]
                 + [cspec(r) for r in range(n_pages)]
                 + [cspec(r) for r in range(n_pages)],
        out_specs=vec,
        scratch_shapes=[pltpu.VMEM((SUBLANES, d), F32),
                        pltpu.VMEM((SUBLANES, d), F32),
                        pltpu.VMEM((SUBLANES, d), F32)],
    )
    return pl.pallas_call(
        functools.partial(_moba_decode_kernel, page=page, past_len=past_len, n_pages=n_pages),
        grid_spec=grid_spec,
        out_shape=jax.ShapeDtypeStruct((b, h, d), F32),
        compiler_params=_cparams(("parallel", "arbitrary")),
        name="moba_decode",
    )(phys_pages, first_pos, slopes, q, ks, vs,
      *([cache_k] * n_pages), *([cache_v] * n_pages))


def _alibi_slopes(n_heads):
    return jnp.asarray([2.0 ** (-8.0 * (h + 1) / n_heads) for h in range(n_heads)], F32)


def _lambda_init_value(layer_idx):
    return 0.8 - 0.6 * math.exp(-0.3 * layer_idx)


def kernel(x_prompt, x_sample, cache_k_l0, cache_v_l0, cache_k_l1, cache_v_l1, page_table,
           w_qkv, w_o, w_gate_up, w_down, g_mix_pre, g_mix_post, g_ffn_pre, g_ffn_post,
           diff_lambda, diff_subln):
    bp, sp, dm = x_prompt.shape
    bs, ds, _ = x_sample.shape
    assert ds == 1
    depth = w_qkv.shape[0]
    page = cache_k_l0.shape[1]
    past_len = page_table.shape[1] * page
    diff_heads = cache_k_l0.shape[2]
    moba_heads = cache_k_l1.shape[2]

    hp = x_prompt.reshape(bp * sp, dm)
    hs = x_sample.reshape(bs * ds, dm)
    wqkv_b = w_qkv.astype(BF16)
    wo_b = w_o.astype(BF16)
    wgu_b = w_gate_up.astype(BF16)
    wdn_b = w_down.astype(BF16)

    rows = []
    for i in range(depth):
        g_pre = g_mix_pre[i].reshape(1, dm)
        qp, kp, vp, kpb, vpt = _norm_qkv(hp, g_pre, wqkv_b, i, tm=512, tn=512, q_dtype=BF16,
                                         tile=ATTN_TILE)
        qs, ks, vs = _norm_qkv(hs, g_pre, wqkv_b, i, tm=bs * ds, tn=512, q_dtype=F32)
        if i % 2 == 0:
            jd = i // 2
            lam_init = _lambda_init_value(i)
            hw = 2 * DIFF_HEAD_DIM
            sl = _alibi_slopes(diff_heads)
            slopes_p = jnp.broadcast_to(sl[:, None, None], (diff_heads, SUBLANES, LANES))
            slopes_s = jnp.broadcast_to(sl[:, None], (diff_heads, hw))
            subln = diff_subln[jd].reshape(1, hw)
            subln_col = jnp.broadcast_to(diff_subln[jd][:, None], (hw, ATTN_TILE))
            op = _diff_prompt(qp, kpb, vpt, slopes_p, diff_lambda[jd], subln_col,
                              batch=bp, seq=sp, heads=diff_heads, lam_init=lam_init,
                              tile=ATTN_TILE)
            os_ = _diff_decode(qs.reshape(bs, diff_heads, hw), ks.reshape(bs, diff_heads, hw),
                               vs.reshape(bs, diff_heads, hw), cache_k_l0, cache_v_l0,
                               page_table, slopes_s, diff_lambda[jd], subln,
                               lam_init=lam_init, pages_per_step=4)
            os_ = os_.reshape(bs * ds, dm)
            rows.append((kp.reshape(bp, sp, diff_heads, hw), vp.reshape(bp, sp, diff_heads, hw),
                         ks.reshape(bs, ds, diff_heads, hw), vs.reshape(bs, ds, diff_heads, hw)))
        else:
            d = MOBA_HEAD_DIM
            sl = _alibi_slopes(moba_heads)
            slopes_p = jnp.broadcast_to(sl[:, None, None], (moba_heads, SUBLANES, LANES))
            slopes_s = jnp.broadcast_to(sl[:, None], (moba_heads, d))
            kmean = _moba_kmean(kp, batch=bp, seq=sp, heads=moba_heads)
            op = _moba_prompt(qp, kpb, vpt, kmean, slopes_p, batch=bp, seq=sp, heads=moba_heads)
            q3 = qs.reshape(bs, moba_heads, d)
            k3 = ks.reshape(bs, moba_heads, d)
            v3 = vs.reshape(bs, moba_heads, d)
            sel = _moba_select(q3, cache_k_l1, page_table,
                               blocks_per_step=2)[:, :, :MOBA_TOPK]
            ppb = MOBA_BLOCK // page
            logical = (sel[..., None] * ppb + jnp.arange(ppb, dtype=jnp.int32)).reshape(bs, -1)
            phys = jnp.take_along_axis(page_table, logical, axis=1)
            os_ = _moba_decode(q3, k3, v3, cache_k_l1, cache_v_l1, phys, logical * page,
                               slopes_s, past_len=past_len)
            os_ = os_.reshape(bs * ds, dm)
            rows.append((kp.reshape(bp, sp, moba_heads, d), vp.reshape(bp, sp, moba_heads, d),
                         ks.reshape(bs, ds, moba_heads, d), vs.reshape(bs, ds, moba_heads, d)))

        g_post = g_mix_post[i].reshape(1, dm)
        hp = _mm_norm_res(op, wo_b, i, g_post, hp, tm=512, tk=dm)
        hs = _mm_norm_res(os_, wo_b, i, g_post, hs, tm=bs * ds, tk=dm)
        g_fpre = g_ffn_pre[i].reshape(1, dm)
        g_fpost = g_ffn_post[i].reshape(1, dm)
        fp = _norm_swiglu(hp, g_fpre, wgu_b, i, tm=1024, tn=512)
        hp = _mm_norm_res(fp, wdn_b, i, g_fpost, hp, tm=1024, tk=512)
        fs = _norm_swiglu(hs, g_fpre, wgu_b, i, tm=bs * ds, tn=512)
        hs = _mm_norm_res(fs, wdn_b, i, g_fpost, hs, tm=bs * ds, tk=512)

    (k0p, v0p, k0s, v0s), (k1p, v1p, k1s, v1s) = rows
    return (hp.reshape(bp, sp, dm), hs.reshape(bs, ds, dm),
            k0p, v0p, k0s, v0s, k1p, v1p, k1s, v1s)
```

```python
import functools
import math

import jax
import jax.numpy as jnp
from jax import lax
from jax.experimental import pallas as pl
from jax.experimental.pallas import tpu as pltpu

F32 = jnp.float32
BF16 = jnp.bfloat16

NORM_EPS = 1e-6
NEG_INF = -1e30
LOG2E = math.log2(math.e)
DIFF_HEAD_DIM = 128
MOBA_HEAD_DIM = 128
MOBA_BLOCK = 256
MOBA_TOPK = 3
ATTN_TILE = MOBA_BLOCK
KV_GROUP = 2

LANES = 128
SUBLANES = 8
VMEM_LIMIT = 56 * 1024 * 1024


def _cparams(sem):
    return pltpu.CompilerParams(dimension_semantics=sem, vmem_limit_bytes=VMEM_LIMIT)


def _nt_dot(a, b):
    return lax.dot_general(a, b, (((1,), (1,)), ((), ())), preferred_element_type=F32)


def _rms_scale(x):
    return lax.rsqrt(jnp.mean(x * x, axis=-1, keepdims=True) + NORM_EPS)


def _norm_qkv_kernel(x_ref, g_ref, wq_ref, wk_ref, wv_ref, q_ref, k_ref, v_ref, *rest,
                     tile, q_scale):
    xn_ref = rest[-1]

    @pl.when(pl.program_id(1) == 0)
    def _():
        x = x_ref[...]
        xn_ref[...] = ((x * _rms_scale(x)) * g_ref[...]).astype(BF16)

    xn = xn_ref[...]
    q = jnp.dot(xn, wq_ref[...], preferred_element_type=F32)
    k = jnp.dot(xn, wk_ref[...], preferred_element_type=F32)
    v = jnp.dot(xn, wv_ref[...], preferred_element_type=F32)
    q_ref[...] = (q if q_scale is None else q * q_scale).astype(q_ref.dtype)
    k_ref[...] = k
    v_ref[...] = v
    if len(rest) == 3:
        kb_ref, vt_ref = rest[:2]
        kb_ref[...] = k.astype(BF16)
        for u in range(vt_ref.shape[0]):
            vt_ref[u] = v[u * tile:(u + 1) * tile, :].T.astype(BF16)


def _norm_qkv(x, g, w, layer, *, tm, tn, q_dtype, tile=None, q_scale=None):
    m, d = x.shape
    nj = d // tn
    wspec = lambda off: pl.BlockSpec((None, d, tn), lambda i, j: (layer, 0, j + off))
    ospec = pl.BlockSpec((tm, tn), lambda i, j: (i, j))
    out_specs = [ospec] * 3
    out_shape = [jax.ShapeDtypeStruct((m, d), q_dtype),
                 jax.ShapeDtypeStruct((m, d), F32),
                 jax.ShapeDtypeStruct((m, d), F32)]
    if tile is not None:
        out_specs += [ospec, pl.BlockSpec((tm // tile, tn, tile), lambda i, j: (i, j, 0))]
        out_shape += [jax.ShapeDtypeStruct((m, d), BF16),
                      jax.ShapeDtypeStruct((m // tile, d, tile), BF16)]
    return pl.pallas_call(
        functools.partial(_norm_qkv_kernel, tile=tile, q_scale=q_scale),
        grid=(m // tm, nj),
        in_specs=[pl.BlockSpec((tm, d), lambda i, j: (i, 0)),
                  pl.BlockSpec((1, d), lambda i, j: (0, 0)),
                  wspec(0), wspec(nj), wspec(2 * nj)],
        out_specs=out_specs,
        out_shape=out_shape,
        scratch_shapes=[pltpu.VMEM((tm, d), BF16)],
        compiler_params=_cparams(("parallel", "arbitrary")),
        name="norm_qkv",
    )(x, g, w, w, w)


def _norm_swiglu_kernel(x_ref, g_ref, wg_ref, wu_ref, h_ref, xn_ref):
    @pl.when(pl.program_id(1) == 0)
    def _():
        x = x_ref[...]
        xn_ref[...] = ((x * _rms_scale(x)) * g_ref[...]).astype(BF16)

    xn = xn_ref[...]
    gate = jnp.dot(xn, wg_ref[...], preferred_element_type=F32)
    up = jnp.dot(xn, wu_ref[...], preferred_element_type=F32)
    h_ref[...] = ((gate * (1.0 / (1.0 + jnp.exp(-gate)))) * up).astype(BF16)


def _norm_swiglu(x, g, w, layer, *, tm, tn):
    m, d = x.shape
    f = w.shape[2] // 2
    nj = f // tn
    return pl.pallas_call(
        _norm_swiglu_kernel,
        grid=(m // tm, nj),
        in_specs=[pl.BlockSpec((tm, d), lambda i, j: (i, 0)),
                  pl.BlockSpec((1, d), lambda i, j: (0, 0)),
                  pl.BlockSpec((None, d, tn), lambda i, j: (layer, 0, j)),
                  pl.BlockSpec((None, d, tn), lambda i, j: (layer, 0, j + nj))],
        out_specs=pl.BlockSpec((tm, tn), lambda i, j: (i, j)),
        out_shape=jax.ShapeDtypeStruct((m, f), BF16),
        scratch_shapes=[pltpu.VMEM((tm, d), BF16)],
        compiler_params=_cparams(("parallel", "arbitrary")),
        name="norm_swiglu",
    )(x, g, w, w)


def _mm_norm_res_kernel(a_ref, w_ref, g_ref, r_ref, o_ref, acc_ref):
    kk = pl.program_id(1)

    @pl.when(kk == 0)
    def _():
        acc_ref[...] = jnp.zeros_like(acc_ref)

    acc_ref[...] += jnp.dot(a_ref[...].astype(BF16), w_ref[...],
                            preferred_element_type=F32)

    @pl.when(kk == pl.num_programs(1) - 1)
    def _():
        y = acc_ref[...]
        o_ref[...] = r_ref[...] + (y * _rms_scale(y)) * g_ref[...]


def _mm_norm_res(a, w, layer, g, res, *, tm, tk):
    m, kdim = a.shape
    n = w.shape[2]
    return pl.pallas_call(
        _mm_norm_res_kernel,
        grid=(m // tm, kdim // tk),
        in_specs=[pl.BlockSpec((tm, tk), lambda i, k: (i, k)),
                  pl.BlockSpec((None, tk, n), lambda i, k: (layer, k, 0)),
                  pl.BlockSpec((1, n), lambda i, k: (0, 0)),
                  pl.BlockSpec((tm, n), lambda i, k: (i, 0))],
        out_specs=pl.BlockSpec((tm, n), lambda i, k: (i, 0)),
        out_shape=jax.ShapeDtypeStruct((m, n), F32),
        scratch_shapes=[pltpu.VMEM((tm, n), F32)],
        compiler_params=_cparams(("parallel", "arbitrary")),
        name="mm_norm_res",
    )(a, w, g, res)


def _diff_lambda(lp, lam_init):
    a = jnp.sum(lp[0:1] * lp[1:2], axis=-1, keepdims=True)
    b = jnp.sum(lp[2:3] * lp[3:4], axis=-1, keepdims=True)
    return jnp.exp(a) - jnp.exp(b) + lam_init


def _alibi_operands(slope2, tile):
    assert tile <= 256
    lane = lax.broadcasted_iota(jnp.int32, (tile, LANES), 1)
    row = lax.broadcasted_iota(jnp.int32, (tile, LANES), 0).astype(F32)
    k_side = jnp.where(lane < 3, row, 0.0).astype(BF16)
    hi = slope2.astype(BF16).astype(F32)
    mid = (slope2 - hi).astype(BF16).astype(F32)
    lo = (slope2 - hi - mid).astype(BF16).astype(F32)
    q_side = jnp.where(lane == 0, hi, jnp.where(lane == 1, mid, jnp.where(lane == 2, lo, 0.0)))
    return k_side, q_side.astype(BF16)


def _diff_prompt_kernel(slope_ref, lam_ref, subln_ref, q_ref, k_ref, vt_ref, o_ref,
                        acc_ref, sa_ref, sb_ref, *, tile, lam_init):
    i = pl.program_id(2)
    dk = DIFF_HEAD_DIM
    hw = 2 * dk
    gk = KV_GROUP * tile
    last_group = k_ref.shape[0] // gk - 1
    n_heads = slope_ref.shape[0]
    chains = [(h, c) for h in range(n_heads) for c in range(2)]
    krow = lax.broadcasted_iota(jnp.int32, (tile, tile), 0)
    qcol = lax.broadcasted_iota(jnp.int32, (tile, tile), 1)
    slope2 = [slope_ref[h][0:1, 0:1] * LOG2E for h in range(n_heads)]
    sides = [_alibi_operands(slope2[h], tile) for h in range(n_heads)]
    k_side = sides[0][0]
    k_side_g = jnp.concatenate([k_side] * KV_GROUP, axis=0)
    q_aug = {(h, c): jnp.concatenate([q_ref[:, h * hw + c * dk:h * hw + (c + 1) * dk],
                                      sides[h][1]], axis=-1) for h, c in chains}
    acc_ref[...] = jnp.zeros_like(acc_ref)

    def scores(k, side, h, c):
        return _nt_dot(jnp.concatenate([k[:, h * hw + c * dk:h * hw + (c + 1) * dk], side],
                                       axis=-1), q_aug[h, c])

    def produce(s_ref, g):
        k = k_ref[pl.ds(pl.multiple_of(g * gk, gk), gk), :]
        for n, (h, c) in enumerate(chains):
            s = scores(k, k_side_g, h, c)
            for u in range(KV_GROUP):
                s_ref[n, u] = s[u * tile:(u + 1) * tile]

    def update(tiles, get, shifts, state):
        out = []
        for n, (h, c) in enumerate(chains):
            m_prev, l_prev = state[2 * n], state[2 * n + 1]
            m_new = m_prev
            for u, shift in enumerate(shifts[h]):
                m_new = jnp.maximum(m_new, jnp.max(get(n, u), axis=0, keepdims=True) + shift)
            alpha = jnp.exp2(m_prev - m_new)
            l_new = alpha * l_prev
            pv = None
            for u, (t, shift) in enumerate(zip(tiles, shifts[h])):
                p = jnp.exp2(get(n, u) - (m_new - shift))
                l_new = l_new + jnp.sum(p, axis=0, keepdims=True)
                part = jnp.dot(vt_ref[t, h * hw:(h + 1) * hw, :], p.astype(BF16),
                               preferred_element_type=F32)
                pv = part if pv is None else pv + part
            acc_ref[n] = alpha * acc_ref[n] + pv
            out += [m_new, l_new]
        return tuple(out)

    def consume(s_ref, g, state):
        tiles = [g * KV_GROUP + u for u in range(KV_GROUP)]
        shifts = [[jnp.where(t < i, -slope2[h] * ((i - t) * tile).astype(F32), NEG_INF)
                   for t in tiles] for h in range(n_heads)]
        return update(tiles, lambda n, u: s_ref[n, u], shifts, state)

    def body(jj, state):
        produce(sb_ref, 2 * jj + 1)
        state = consume(sa_ref, 2 * jj, state)
        produce(sa_ref, jnp.minimum(2 * jj + 2, last_group))
        return consume(sb_ref, 2 * jj + 1, state)

    neg = jnp.full((1, tile), NEG_INF, F32)
    zero = jnp.zeros((1, tile), F32)
    produce(sa_ref, 0)
    n_pairs = (i + 2 * KV_GROUP - 1) // (2 * KV_GROUP)
    state = lax.fori_loop(0, n_pairs, body, (neg, zero) * len(chains))
    k_diag = k_ref[pl.ds(pl.multiple_of(i * tile, tile), tile), :]
    diag = [jnp.where(qcol >= krow, scores(k_diag, k_side, h, c), NEG_INF) for h, c in chains]
    no_shift = [[jnp.zeros((1, 1), F32)]] * n_heads
    state = update([i], lambda n, u: diag[n], no_shift, state)

    lam = _diff_lambda(lam_ref[...], lam_init)
    for h in range(n_heads):
        l0, l1 = state[4 * h + 1], state[4 * h + 3]
        o = acc_ref[2 * h] * (1.0 / l0) - lam * (acc_ref[2 * h + 1] * (1.0 / l1))
        inv_rms = lax.rsqrt(jnp.mean(o * o, axis=0, keepdims=True) + NORM_EPS)
        o = ((o * inv_rms) * subln_ref[...]) * (1.0 - lam_init)
        o_ref[:, h * hw:(h + 1) * hw] = o.T.astype(o_ref.dtype)


def _diff_prompt(q, k, vt, slopes, lam_params, subln_col, *, batch, seq, heads, lam_init, tile,
                 heads_per_step):
    hw = 2 * DIFF_HEAD_DIM
    nq = seq // tile
    hp = heads_per_step
    kern = functools.partial(_diff_prompt_kernel, tile=tile, lam_init=lam_init)
    return pl.pallas_call(
        kern,
        grid=(batch, heads // hp, nq),
        in_specs=[pl.BlockSpec((hp, SUBLANES, LANES), lambda b, h, i: (h, 0, 0)),
                  pl.BlockSpec((4, DIFF_HEAD_DIM), lambda b, h, i: (0, 0)),
                  pl.BlockSpec((hw, tile), lambda b, h, i: (0, 0)),
                  pl.BlockSpec((tile, hp * hw), lambda b, h, i: (b * nq + i, h)),
                  pl.BlockSpec((seq, hp * hw), lambda b, h, i: (b, h)),
                  pl.BlockSpec((nq, hp * hw, tile), lambda b, h, i: (b, h, 0))],
        out_specs=pl.BlockSpec((tile, hp * hw), lambda b, h, i: (b * nq + i, h)),
        out_shape=jax.ShapeDtypeStruct(q.shape, BF16),
        scratch_shapes=[pltpu.VMEM((2 * hp, hw, tile), F32),
                        pltpu.VMEM((2 * hp, KV_GROUP, tile, tile), F32),
                        pltpu.VMEM((2 * hp, KV_GROUP, tile, tile), F32)],
        compiler_params=_cparams(("parallel", "parallel", "arbitrary")),
        name="diff_prompt",
    )(slopes, lam_params, subln_col, q, k, vt)


def _group_ones(width):
    r = lax.broadcasted_iota(jnp.int32, (width, width), 0) // LANES
    c = lax.broadcasted_iota(jnp.int32, (width, width), 1) // LANES
    return (r == c).astype(BF16)


def _lane_group_sums(x, ones):
    shape = x.shape
    x2 = x.reshape((-1, shape[-1])).astype(BF16)
    return jnp.dot(x2, ones, preferred_element_type=F32).reshape(shape)


def _twice(x):
    return jnp.concatenate([x, x], axis=-1)


def _diff_decode_kernel(pt_ref, slope_ref, lam_ref, subln_ref, q_ref, ks_ref, vs_ref, *rest,
                        pages_per_step, page, past_len, lam_init):
    npg = pages_per_step
    k_refs = rest[:npg]
    v_refs = rest[npg:2 * npg]
    o_ref = rest[2 * npg]
    m_ref, l_ref, acc_ref, tb_ref = rest[2 * npg + 1:]
    j = pl.program_id(1)
    dk = DIFF_HEAD_DIM
    q2 = q_ref[0] * (dk ** -0.5 * LOG2E)
    slope2 = slope_ref[...] * LOG2E
    ones = _group_ones(2 * dk)

    @pl.when(j == 0)
    def _():
        m_ref[...] = jnp.full_like(m_ref, NEG_INF)
        l_ref[...] = jnp.zeros_like(l_ref)
        acc_ref[...] = jnp.zeros_like(acc_ref)
        t_idx = lax.broadcasted_iota(jnp.int32, tb_ref.shape, 0)
        tb_ref[...] = slope2[None] * t_idx.astype(F32)

    def update(ss, shifts, vv):
        m_prev = m_ref[...]
        m_new = m_prev
        for s, shift in zip(ss, shifts):
            m_new = jnp.maximum(m_new, jnp.max(s, axis=0) + shift)
        alpha = jnp.exp2(m_prev - m_new)
        l_new = alpha * l_ref[...]
        acc_new = [_twice(alpha[:, c * dk:(c + 1) * dk]) * acc_ref[c] for c in range(2)]
        for s, shift, v in zip(ss, shifts, vv):
            p = jnp.exp2(s - (m_new - shift)[None])
            l_new = l_new + jnp.sum(p, axis=0)
            for c in range(2):
                pc = _twice(p[:, :, c * dk:(c + 1) * dk])
                acc_new[c] = acc_new[c] + jnp.sum(pc * v, axis=0)
        l_ref[...] = l_new
        for c in range(2):
            acc_ref[c] = acc_new[c]
        m_ref[...] = m_new

    ss, shifts = [], []
    for pg in range(npg):
        base = (j * npg + pg) * page
        shifts.append(-slope2 * (past_len - base).astype(F32))
        ss.append(_lane_group_sums(k_refs[pg][0] * q2[None], ones) + tb_ref[...])
    update(ss, shifts, [v_refs[pg][0] for pg in range(npg)])

    @pl.when(j == pl.num_programs(1) - 1)
    def _():
        s_self = _lane_group_sums(ks_ref[0] * q2, ones)
        update([s_self[None]], [jnp.zeros_like(slope2)], [vs_ref[0][None]])
        lam = _diff_lambda(lam_ref[...], lam_init)
        l = l_ref[...]
        o = (acc_ref[0] / _twice(l[:, :dk])) - lam * (acc_ref[1] / _twice(l[:, dk:]))
        o = ((o * _rms_scale(o)) * subln_ref[...]) * (1.0 - lam_init)
        o_ref[0] = o


def _diff_decode(q, ks, vs, cache_k, cache_v, page_table, slopes, lam_params, subln, *,
                 lam_init, pages_per_step):
    b, h, w = q.shape
    page = cache_k.shape[1]
    n_pages = page_table.shape[1]
    npg = pages_per_step
    kern = functools.partial(_diff_decode_kernel, pages_per_step=npg, page=page,
                             past_len=n_pages * page, lam_init=lam_init)

    def cache_spec(pg):
        return pl.BlockSpec((1, page, h, w),
                            lambda bi, j, pt: (pt[bi, j * npg + pg], 0, 0, 0))

    vec = pl.BlockSpec((1, h, w), lambda bi, j, pt: (bi, 0, 0))
    grid_spec = pltpu.PrefetchScalarGridSpec(
        num_scalar_prefetch=1,
        grid=(b, n_pages // npg),
        in_specs=[pl.BlockSpec((h, w), lambda bi, j, pt: (0, 0)),
                  pl.BlockSpec((4, DIFF_HEAD_DIM), lambda bi, j, pt: (0, 0)),
                  pl.BlockSpec((1, w), lambda bi, j, pt: (0, 0)),
                  vec, vec, vec]
                 + [cache_spec(pg) for pg in range(npg)]
                 + [cache_spec(pg) for pg in range(npg)],
        out_specs=vec,
        scratch_shapes=[pltpu.VMEM((h, w), F32),
                        pltpu.VMEM((h, w), F32),
                        pltpu.VMEM((2, h, w), F32),
                        pltpu.VMEM((page, h, w), F32)],
    )
    return pl.pallas_call(
        kern,
        grid_spec=grid_spec,
        out_shape=jax.ShapeDtypeStruct((b, h, w), F32),
        compiler_params=_cparams(("parallel", "arbitrary")),
        name="diff_decode",
    )(page_table, slopes, lam_params, subln, q, ks, vs,
      *([cache_k] * npg), *([cache_v] * npg))


def _moba_kmean_kernel(k_ref, o_ref, *, nblk):
    k = k_ref[...]
    o_ref[0, 0] = jnp.mean(k.reshape(nblk, MOBA_BLOCK, k.shape[-1]), axis=1)


def _moba_kmean(k, *, batch, seq, heads):
    d = MOBA_HEAD_DIM
    nblk = seq // MOBA_BLOCK
    return pl.pallas_call(
        functools.partial(_moba_kmean_kernel, nblk=nblk),
        grid=(batch, heads),
        in_specs=[pl.BlockSpec((seq, d), lambda b, h: (b, h))],
        out_specs=pl.BlockSpec((1, 1, nblk, d), lambda b, h: (b, h, 0, 0)),
        out_shape=jax.ShapeDtypeStruct((batch, heads, nblk, d), F32),
        compiler_params=_cparams(("parallel", "parallel")),
        name="moba_kmean",
    )(k)


def _top_blocks(gate, n_valid, idx, axis):
    valid = idx < n_valid
    g = jnp.where(valid, gate, NEG_INF)
    chosen = jnp.zeros(gate.shape, F32)
    past_end = float(gate.shape[axis])
    for _ in range(MOBA_TOPK):
        mx = jnp.max(g, axis=axis, keepdims=True)
        first = jnp.min(jnp.where(g == mx, idx, past_end), axis=axis, keepdims=True)
        pick = idx == first
        chosen = jnp.where(pick & valid, 1.0, chosen)
        g = jnp.where(pick, -jnp.inf, g)
    return chosen


def _moba_prompt_kernel(slope_ref, km_ref, q_ref, k_ref, vt_ref, o_ref, acc_ref, gb_ref,
                        sa_ref, sb_ref):
    i = pl.program_id(2)
    d = MOBA_HEAD_DIM
    blk = MOBA_BLOCK
    gk = KV_GROUP * blk
    last_group = k_ref.shape[0] // gk - 1
    n_heads = slope_ref.shape[0]
    heads = range(n_heads)
    krow = lax.broadcasted_iota(jnp.int32, (blk, blk), 0)
    qcol = lax.broadcasted_iota(jnp.int32, (blk, blk), 1)
    slope2 = [slope_ref[h][0:1, 0:1] * LOG2E for h in heads]
    sides = [_alibi_operands(slope2[h], blk) for h in heads]
    k_side = sides[0][0]
    k_side_g = jnp.concatenate([k_side] * KV_GROUP, axis=0)
    q = [q_ref[:, h * d:(h + 1) * d] for h in heads]
    q_aug = [jnp.concatenate([q[h], sides[h][1]], axis=-1) for h in heads]

    def scores(k, side, h):
        return _nt_dot(jnp.concatenate([k[:, h * d:(h + 1) * d], side], axis=-1), q_aug[h])

    for h in heads:
        km = km_ref[0, h]
        km_hi = km.astype(BF16)
        km_lo = (km - km_hi.astype(F32)).astype(BF16)
        gate = _nt_dot(km_hi, q[h]) + _nt_dot(km_lo, q[h])
        nrow = lax.broadcasted_iota(jnp.int32, gate.shape, 0).astype(F32)
        chosen = _top_blocks(gate, i.astype(F32), nrow, 0)
        gb_ref[h] = jnp.where(chosen > 0.5, 0.0, NEG_INF)

    k_own = k_ref[pl.ds(pl.multiple_of(i * blk, blk), blk), :]
    state = []
    for h in heads:
        s = jnp.where(qcol >= krow, scores(k_own, k_side, h), NEG_INF)
        m0 = jnp.max(s, axis=0, keepdims=True)
        p = jnp.exp2(s - m0)
        acc_ref[h] = jnp.dot(vt_ref[i, h * d:(h + 1) * d, :], p.astype(BF16),
                             preferred_element_type=F32)
        state += [m0, jnp.sum(p, axis=0, keepdims=True)]

    def produce(s_ref, g):
        k = k_ref[pl.ds(pl.multiple_of(g * gk, gk), gk), :]
        for h in heads:
            s = scores(k, k_side_g, h)
            for u in range(KV_GROUP):
                s_ref[h, u] = s[u * blk:(u + 1) * blk]

    def consume(s_ref, g, state):
        blocks = [g * KV_GROUP + u for u in range(KV_GROUP)]
        out = []
        for h in heads:
            m_prev, l_prev = state[2 * h], state[2 * h + 1]
            shifts = [gb_ref[h, pl.ds(n, 1), :] - slope2[h] * ((i - n) * blk).astype(F32)
                      for n in blocks]
            m_new = m_prev
            for u, shift in enumerate(shifts):
                m_new = jnp.maximum(m_new, jnp.max(s_ref[h, u], axis=0, keepdims=True) + shift)
            alpha = jnp.exp2(m_prev - m_new)
            l_new = alpha * l_prev
            pv = None
            for u, (n, shift) in enumerate(zip(blocks, shifts)):
                p = jnp.exp2(s_ref[h, u] - (m_new - shift))
                l_new = l_new + jnp.sum(p, axis=0, keepdims=True)
                part = jnp.dot(vt_ref[n, h * d:(h + 1) * d, :], p.astype(BF16),
                               preferred_element_type=F32)
                pv = part if pv is None else pv + part
            acc_ref[h] = alpha * acc_ref[h] + pv
            out += [m_new, l_new]
        return tuple(out)

    def body(jj, state):
        produce(sb_ref, 2 * jj + 1)
        state = consume(sa_ref, 2 * jj, state)
        produce(sa_ref, jnp.minimum(2 * jj + 2, last_group))
        return consume(sb_ref, 2 * jj + 1, state)

    produce(sa_ref, 0)
    n_pairs = (i + 2 * KV_GROUP - 1) // (2 * KV_GROUP)
    state = lax.fori_loop(0, n_pairs, body, tuple(state))
    for h in heads:
        o_ref[:, h * d:(h + 1) * d] = (acc_ref[h] * (1.0 / state[2 * h + 1])).T.astype(o_ref.dtype)


def _moba_prompt(q, k, vt, kmean, slopes, *, batch, seq, heads, heads_per_step):
    d = MOBA_HEAD_DIM
    blk = MOBA_BLOCK
    nq = seq // blk
    hp = heads_per_step
    return pl.pallas_call(
        _moba_prompt_kernel,
        grid=(batch, heads // hp, nq),
        in_specs=[pl.BlockSpec((hp, SUBLANES, LANES), lambda b, h, i: (h, 0, 0)),
                  pl.BlockSpec((1, hp, nq, d), lambda b, h, i: (b, h, 0, 0)),
                  pl.BlockSpec((blk, hp * d), lambda b, h, i: (b * nq + i, h)),
                  pl.BlockSpec((seq, hp * d), lambda b, h, i: (b, h)),
                  pl.BlockSpec((nq, hp * d, blk), lambda b, h, i: (b, h, 0))],
        out_specs=pl.BlockSpec((blk, hp * d), lambda b, h, i: (b * nq + i, h)),
        out_shape=jax.ShapeDtypeStruct(q.shape, BF16),
        scratch_shapes=[pltpu.VMEM((hp, d, blk), F32),
                        pltpu.VMEM((hp, nq, blk), F32),
                        pltpu.VMEM((hp, KV_GROUP, blk, blk), F32),
                        pltpu.VMEM((hp, KV_GROUP, blk, blk), F32)],
        compiler_params=_cparams(("parallel", "parallel", "arbitrary")),
        name="moba_prompt",
    )(slopes, kmean, q, k, vt)


def _moba_select_kernel(pt_ref, q_ref, *rest, n_blocks, blocks_per_step, pages_per_block):
    n_in = blocks_per_step * pages_per_block
    k_refs = rest[:n_in]
    sel_ref, gate_ref = rest[n_in:]
    step = pl.program_id(1)
    lane = lax.broadcasted_iota(jnp.int32, gate_ref.shape, 1)

    @pl.when(step == 0)
    def _():
        gate_ref[...] = jnp.full_like(gate_ref, NEG_INF)

    gates = gate_ref[...]
    for u in range(blocks_per_step):
        ksum = jnp.sum(k_refs[u * pages_per_block][0], axis=0)
        for pg in range(1, pages_per_block):
            ksum = ksum + jnp.sum(k_refs[u * pages_per_block + pg][0], axis=0)
        g = jnp.sum(q_ref[0] * (ksum * (1.0 / MOBA_BLOCK)), axis=-1, keepdims=True)
        gates = jnp.where(lane == step * blocks_per_step + u, g, gates)
    gate_ref[...] = gates

    @pl.when(step == pl.num_programs(1) - 1)
    def _():
        lane_f = lane.astype(F32)
        cand = jnp.where(lane < n_blocks, gate_ref[...], NEG_INF)
        sel = jnp.zeros(cand.shape, F32)
        for r in range(MOBA_TOPK):
            mx = jnp.max(cand, axis=-1, keepdims=True)
            first = jnp.min(jnp.where(cand == mx, lane_f, float(LANES)), axis=-1, keepdims=True)
            sel = jnp.where(lane == r, first, sel)
            cand = jnp.where(lane_f == first, -jnp.inf, cand)
        sel_ref[0] = sel.astype(jnp.int32)


def _moba_select(q, cache_k, page_table, *, blocks_per_step):
    b, h, d = q.shape
    page = cache_k.shape[1]
    ppb = MOBA_BLOCK // page
    n_blocks = page_table.shape[1] // ppb
    assert n_blocks <= LANES and n_blocks % blocks_per_step == 0
    n_in = blocks_per_step * ppb

    def kspec(u):
        return pl.BlockSpec((1, page, h, d), lambda bi, n, pt: (pt[bi, n * n_in + u], 0, 0, 0))

    grid_spec = pltpu.PrefetchScalarGridSpec(
        num_scalar_prefetch=1,
        grid=(b, n_blocks // blocks_per_step),
        in_specs=[pl.BlockSpec((1, h, d), lambda bi, n, pt: (bi, 0, 0))]
                 + [kspec(u) for u in range(n_in)],
        out_specs=pl.BlockSpec((1, h, LANES), lambda bi, n, pt: (bi, 0, 0)),
        scratch_shapes=[pltpu.VMEM((h, LANES), F32)],
    )
    return pl.pallas_call(
        functools.partial(_moba_select_kernel, n_blocks=n_blocks,
                          blocks_per_step=blocks_per_step, pages_per_block=ppb),
        grid_spec=grid_spec,
        out_shape=jax.ShapeDtypeStruct((b, h, LANES), jnp.int32),
        compiler_params=_cparams(("parallel", "arbitrary")),
        name="moba_select",
    )(page_table, q, *([cache_k] * n_in))


def _moba_decode_kernel(pg_ref, pos_ref, slope_ref, q_ref, ks_ref, vs_ref, *rest,
                        page, past_len, n_pages):
    k_refs = rest[:n_pages]
    v_refs = rest[n_pages:2 * n_pages]
    o_ref = rest[2 * n_pages]
    m_ref, l_ref, acc_ref = rest[2 * n_pages + 1:]
    bi = pl.program_id(0)
    h = pl.program_id(1)
    hl = h % SUBLANES
    d = MOBA_HEAD_DIM
    scale = d ** -0.5
    q = q_ref[0]
    slope = slope_ref[...]
    ones = _group_ones(d)

    @pl.when(hl == 0)
    def _():
        m_ref[...] = jnp.full_like(m_ref, NEG_INF)
        l_ref[...] = jnp.zeros_like(l_ref)
        acc_ref[...] = jnp.zeros_like(acc_ref)

    def update(ss, vv, live):
        m_prev = m_ref[...]
        m_all = m_prev
        for s in ss:
            m_all = jnp.maximum(m_all, jnp.max(s, axis=0))
        m_new = jnp.where(live, m_all, m_prev)
        alpha = jnp.exp(m_prev - m_new)
        l_new = alpha * l_ref[...]
        acc_new = alpha * acc_ref[...]
        for s, v in zip(ss, vv):
            p = jnp.where(live[None], jnp.exp(s - m_new[None]), 0.0)
            l_new = l_new + jnp.sum(p, axis=0)
            acc_new = acc_new + jnp.sum(p * v, axis=0)
        l_ref[...] = l_new
        acc_ref[...] = acc_new
        m_ref[...] = m_new

    t_idx = lax.broadcasted_iota(jnp.int32, (page, SUBLANES, d), 0)
    ss = []
    for r in range(n_pages):
        kpos = pos_ref[bi, h * n_pages + r] + t_idx
        dist = (past_len - kpos).astype(F32)
        ss.append(_lane_group_sums(k_refs[r][0] * q[None], ones) * scale - slope[None] * dist)
    row = lax.broadcasted_iota(jnp.int32, (SUBLANES, d), 0)
    update(ss, [v_refs[r][0] for r in range(n_pages)], row == hl)

    @pl.when(hl == SUBLANES - 1)
    def _():
        s_self = _lane_group_sums(ks_ref[0] * q, ones) * scale
        update([s_self[None]], [vs_ref[0][None]], row >= 0)
        o_ref[0] = acc_ref[...] / l_ref[...]


def _moba_decode(q, ks, vs, cache_k, cache_v, phys_pages, first_pos, slopes, *, past_len):
    b, h, d = q.shape
    page = cache_k.shape[1]
    n_pages = phys_pages.shape[1] // h

    def cspec(r):
        return pl.BlockSpec((1, page, SUBLANES, d),
                            lambda bi, hh, pg, pos: (pg[bi, hh * n_pages + r], 0, hh // SUBLANES, 0))

    vec = pl.BlockSpec((1, SUBLANES, d), lambda bi, hh, pg, pos: (bi, hh // SUBLANES, 0))
    grid_spec = pltpu.PrefetchScalarGridSpec(
        num_scalar_prefetch=2,
        grid=(b, h),
        in_specs=[pl.BlockSpec((SUBLANES, d), lambda bi, hh, pg, pos: (hh // SUBLANES, 0)),
                  vec, vec, vec]
                 + [cspec(r) for r in range(n_pages)]
                 + [cspec(r) for r in range(n_pages)],
        out_specs=vec,
        scratch_shapes=[pltpu.VMEM((SUBLANES, d), F32),
                        pltpu.VMEM((SUBLANES, d), F32),
                        pltpu.VMEM((SUBLANES, d), F32)],
    )
    return pl.pallas_call(
        functools.partial(_moba_decode_kernel, page=page, past_len=past_len, n_pages=n_pages),
        grid_spec=grid_spec,
        out_shape=jax.ShapeDtypeStruct((b, h, d), F32),
        compiler_params=_cparams(("parallel", "arbitrary")),
        name="moba_decode",
    )(phys_pages, first_pos, slopes, q, ks, vs,
      *([cache_k] * n_pages), *([cache_v] * n_pages))


def _alibi_slopes(n_heads):
    return jnp.asarray([2.0 ** (-8.0 * (h + 1) / n_heads) for h in range(n_heads)], F32)


def _lambda_init_value(layer_idx):
    return 0.8 - 0.6 * math.exp(-0.3 * layer_idx)


def kernel(x_prompt, x_sample, cache_k_l0, cache_v_l0, cache_k_l1, cache_v_l1, page_table,
           w_qkv, w_o, w_gate_up, w_down, g_mix_pre, g_mix_post, g_ffn_pre, g_ffn_post,
           diff_lambda, diff_subln):
    bp, sp, dm = x_prompt.shape
    bs, ds, _ = x_sample.shape
    assert ds == 1
    depth = w_qkv.shape[0]
    page = cache_k_l0.shape[1]
    past_len = page_table.shape[1] * page
    diff_heads = cache_k_l0.shape[2]
    moba_heads = cache_k_l1.shape[2]

    hp = x_prompt.reshape(bp * sp, dm)
    hs = x_sample.reshape(bs * ds, dm)
    wqkv_b = w_qkv.astype(BF16)
    wo_b = w_o.astype(BF16)
    wgu_b = w_gate_up.astype(BF16)
    wdn_b = w_down.astype(BF16)

    rows = []
    for i in range(depth):
        g_pre = g_mix_pre[i].reshape(1, dm)
        head_dim = DIFF_HEAD_DIM if i % 2 == 0 else MOBA_HEAD_DIM
        qp, kp, vp, kpb, vpt = _norm_qkv(hp, g_pre, wqkv_b, i, tm=512, tn=512, q_dtype=BF16,
                                         tile=ATTN_TILE, q_scale=head_dim ** -0.5 * LOG2E)
        qs, ks, vs = _norm_qkv(hs, g_pre, wqkv_b, i, tm=bs * ds, tn=512, q_dtype=F32)
        if i % 2 == 0:
            jd = i // 2
            lam_init = _lambda_init_value(i)
            hw = 2 * DIFF_HEAD_DIM
            sl = _alibi_slopes(diff_heads)
            slopes_p = jnp.broadcast_to(sl[:, None, None], (diff_heads, SUBLANES, LANES))
            slopes_s = jnp.broadcast_to(sl[:, None], (diff_heads, hw))
            subln = diff_subln[jd].reshape(1, hw)
            subln_col = jnp.broadcast_to(diff_subln[jd][:, None], (hw, ATTN_TILE))
            op = _diff_prompt(qp, kpb, vpt, slopes_p, diff_lambda[jd], subln_col,
                              batch=bp, seq=sp, heads=diff_heads, lam_init=lam_init,
                              tile=ATTN_TILE, heads_per_step=2)
            os_ = _diff_decode(qs.reshape(bs, diff_heads, hw), ks.reshape(bs, diff_heads, hw),
                               vs.reshape(bs, diff_heads, hw), cache_k_l0, cache_v_l0,
                               page_table, slopes_s, diff_lambda[jd], subln,
                               lam_init=lam_init, pages_per_step=4)
            os_ = os_.reshape(bs * ds, dm)
            rows.append((kp.reshape(bp, sp, diff_heads, hw), vp.reshape(bp, sp, diff_heads, hw),
                         ks.reshape(bs, ds, diff_heads, hw), vs.reshape(bs, ds, diff_heads, hw)))
        else:
            d = MOBA_HEAD_DIM
            sl = _alibi_slopes(moba_heads)
            slopes_p = jnp.broadcast_to(sl[:, None, None], (moba_heads, SUBLANES, LANES))
            slopes_s = jnp.broadcast_to(sl[:, None], (moba_heads, d))
            kmean = _moba_kmean(kp, batch=bp, seq=sp, heads=moba_heads)
            op = _moba_prompt(qp, kpb, vpt, kmean, slopes_p, batch=bp, seq=sp, heads=moba_heads,
                              heads_per_step=4)
            q3 = qs.reshape(bs, moba_heads, d)
            k3 = ks.reshape(bs, moba_heads, d)
            v3 = vs.reshape(bs, moba_heads, d)
            sel = _moba_select(q3, cache_k_l1, page_table,
                               blocks_per_step=2)[:, :, :MOBA_TOPK]
            ppb = MOBA_BLOCK // page
            logical = (sel[..., None] * ppb + jnp.arange(ppb, dtype=jnp.int32)).reshape(bs, -1)
            phys = jnp.take_along_axis(page_table, logical, axis=1)
            os_ = _moba_decode(q3, k3, v3, cache_k_l1, cache_v_l1, phys, logical * page,
                               slopes_s, past_len=past_len)
            os_ = os_.reshape(bs * ds, dm)
            rows.append((kp.reshape(bp, sp, moba_heads, d), vp.reshape(bp, sp, moba_heads, d),
                         ks.reshape(bs, ds, moba_heads, d), vs.reshape(bs, ds, moba_heads, d)))

        g_post = g_mix_post[i].reshape(1, dm)
        hp = _mm_norm_res(op, wo_b, i, g_post, hp, tm=512, tk=dm)
        hs = _mm_norm_res(os_, wo_b, i, g_post, hs, tm=bs * ds, tk=dm)
        g_fpre = g_ffn_pre[i].reshape(1, dm)
        g_fpost = g_ffn_post[i].reshape(1, dm)
        fp = _norm_swiglu(hp, g_fpre, wgu_b, i, tm=1024, tn=512)
        hp = _mm_norm_res(fp, wdn_b, i, g_fpost, hp, tm=1024, tk=512)
        fs = _norm_swiglu(hs, g_fpre, wgu_b, i, tm=bs * ds, tn=512)
        hs = _mm_norm_res(fs, wdn_b, i, g_fpost, hs, tm=bs * ds, tk=512)

    (k0p, v0p, k0s, v0s), (k1p, v1p, k1s, v1s) = rows
    return (hp.reshape(bp, sp, dm), hs.reshape(bs, ds, dm),
            k0p, v0p, k0s, v0s, k1p, v1p, k1s, v1s)
```

```python
import functools
import math

import jax
import jax.numpy as jnp
from jax import lax
from jax.experimental import pallas as pl
from jax.experimental.pallas import tpu as pltpu

F32 = jnp.float32
BF16 = jnp.bfloat16

NORM_EPS = 1e-6
NEG_INF = -1e30
LOG2E = math.log2(math.e)
DIFF_HEAD_DIM = 128
MOBA_HEAD_DIM = 128
MOBA_BLOCK = 256
MOBA_TOPK = 3
ATTN_TILE = MOBA_BLOCK
KV_GROUP = 2

LANES = 128
SUBLANES = 8
VMEM_LIMIT = 56 * 1024 * 1024


def _cparams(sem):
    return pltpu.CompilerParams(dimension_semantics=sem, vmem_limit_bytes=VMEM_LIMIT)


def _nt_dot(a, b):
    return lax.dot_general(a, b, (((1,), (1,)), ((), ())), preferred_element_type=F32)


def _mm(a, w):
    precision = lax.Precision.HIGHEST if w.dtype == F32 else None
    return jnp.dot(a.astype(w.dtype), w, preferred_element_type=F32, precision=precision)


def _rms_scale(x):
    return lax.rsqrt(jnp.mean(x * x, axis=-1, keepdims=True) + NORM_EPS)


def _norm_qkv_kernel(x_ref, g_ref, wq_ref, wk_ref, wv_ref, q_ref, k_ref, v_ref, *rest,
                     tile, q_scale):
    xn_ref = rest[-1]

    @pl.when(pl.program_id(1) == 0)
    def _():
        x = x_ref[...]
        xn_ref[...] = ((x * _rms_scale(x)) * g_ref[...]).astype(xn_ref.dtype)

    xn = xn_ref[...]
    q = _mm(xn, wq_ref[...])
    k = _mm(xn, wk_ref[...])
    v = _mm(xn, wv_ref[...])
    q_ref[...] = (q if q_scale is None else q * q_scale).astype(q_ref.dtype)
    k_ref[...] = k
    v_ref[...] = v
    if len(rest) == 3:
        kb_ref, vt_ref = rest[:2]
        kb_ref[...] = k.astype(BF16)
        for u in range(vt_ref.shape[0]):
            vt_ref[u] = v[u * tile:(u + 1) * tile, :].T.astype(BF16)


def _norm_qkv(x, g, w, layer, *, tm, tn, q_dtype, tile=None, q_scale=None):
    m, d = x.shape
    nj = d // tn
    wspec = lambda off: pl.BlockSpec((None, d, tn), lambda i, j: (layer, 0, j + off))
    ospec = pl.BlockSpec((tm, tn), lambda i, j: (i, j))
    out_specs = [ospec] * 3
    out_shape = [jax.ShapeDtypeStruct((m, d), q_dtype),
                 jax.ShapeDtypeStruct((m, d), F32),
                 jax.ShapeDtypeStruct((m, d), F32)]
    if tile is not None:
        out_specs += [ospec, pl.BlockSpec((tm // tile, tn, tile), lambda i, j: (i, j, 0))]
        out_shape += [jax.ShapeDtypeStruct((m, d), BF16),
                      jax.ShapeDtypeStruct((m // tile, d, tile), BF16)]
    return pl.pallas_call(
        functools.partial(_norm_qkv_kernel, tile=tile, q_scale=q_scale),
        grid=(m // tm, nj),
        in_specs=[pl.BlockSpec((tm, d), lambda i, j: (i, 0)),
                  pl.BlockSpec((1, d), lambda i, j: (0, 0)),
                  wspec(0), wspec(nj), wspec(2 * nj)],
        out_specs=out_specs,
        out_shape=out_shape,
        scratch_shapes=[pltpu.VMEM((tm, d), w.dtype)],
        compiler_params=_cparams(("parallel", "arbitrary")),
        name="norm_qkv",
    )(x, g, w, w, w)


def _norm_swiglu_kernel(x_ref, g_ref, wg_ref, wu_ref, h_ref, xn_ref):
    @pl.when(pl.program_id(1) == 0)
    def _():
        x = x_ref[...]
        xn_ref[...] = ((x * _rms_scale(x)) * g_ref[...]).astype(xn_ref.dtype)

    xn = xn_ref[...]
    gate = _mm(xn, wg_ref[...])
    up = _mm(xn, wu_ref[...])
    h_ref[...] = ((gate * (1.0 / (1.0 + jnp.exp(-gate)))) * up).astype(h_ref.dtype)


def _norm_swiglu(x, g, w, layer, *, tm, tn):
    m, d = x.shape
    f = w.shape[2] // 2
    nj = f // tn
    return pl.pallas_call(
        _norm_swiglu_kernel,
        grid=(m // tm, nj),
        in_specs=[pl.BlockSpec((tm, d), lambda i, j: (i, 0)),
                  pl.BlockSpec((1, d), lambda i, j: (0, 0)),
                  pl.BlockSpec((None, d, tn), lambda i, j: (layer, 0, j)),
                  pl.BlockSpec((None, d, tn), lambda i, j: (layer, 0, j + nj))],
        out_specs=pl.BlockSpec((tm, tn), lambda i, j: (i, j)),
        out_shape=jax.ShapeDtypeStruct((m, f), w.dtype),
        scratch_shapes=[pltpu.VMEM((tm, d), w.dtype)],
        compiler_params=_cparams(("parallel", "arbitrary")),
        name="norm_swiglu",
    )(x, g, w, w)


def _mm_norm_res_kernel(a_ref, w_ref, g_ref, r_ref, o_ref, *acc):
    def finish(y):
        o_ref[...] = r_ref[...] + (y * _rms_scale(y)) * g_ref[...]

    if not acc:
        finish(_mm(a_ref[...], w_ref[...]))
        return

    acc_ref, = acc
    kk = pl.program_id(1)

    @pl.when(kk == 0)
    def _():
        acc_ref[...] = jnp.zeros_like(acc_ref)

    acc_ref[...] += _mm(a_ref[...], w_ref[...])

    @pl.when(kk == pl.num_programs(1) - 1)
    def _():
        finish(acc_ref[...])


def _mm_norm_res(a, w, layer, g, res, *, tm, tk=None):
    m, kdim = a.shape
    n = w.shape[2]
    if tk is None:
        tk = kdim
        w_spec = pl.BlockSpec((None, kdim, n), lambda i, k: (layer, 0, 0),
                              pipeline_mode=pl.Buffered(1))
        scratch = []
    else:
        w_spec = pl.BlockSpec((None, tk, n), lambda i, k: (layer, k, 0))
        scratch = [pltpu.VMEM((tm, n), F32)]
    return pl.pallas_call(
        _mm_norm_res_kernel,
        grid=(m // tm, kdim // tk),
        in_specs=[pl.BlockSpec((tm, tk), lambda i, k: (i, k)),
                  w_spec,
                  pl.BlockSpec((1, n), lambda i, k: (0, 0)),
                  pl.BlockSpec((tm, n), lambda i, k: (i, 0))],
        out_specs=pl.BlockSpec((tm, n), lambda i, k: (i, 0)),
        out_shape=jax.ShapeDtypeStruct((m, n), F32),
        scratch_shapes=scratch,
        compiler_params=_cparams(("parallel", "arbitrary")),
        name="mm_norm_res",
    )(a, w, g, res)


def _diff_lambda(lp, lam_init):
    a = jnp.sum(lp[0:1] * lp[1:2], axis=-1, keepdims=True)
    b = jnp.sum(lp[2:3] * lp[3:4], axis=-1, keepdims=True)
    return jnp.exp(a) - jnp.exp(b) + lam_init


def _alibi_operands(slope2, tile):
    assert tile <= 256
    lane = lax.broadcasted_iota(jnp.int32, (tile, LANES), 1)
    row = lax.broadcasted_iota(jnp.int32, (tile, LANES), 0).astype(F32)
    k_side = jnp.where(lane < 3, row, 0.0).astype(BF16)
    hi = slope2.astype(BF16).astype(F32)
    mid = (slope2 - hi).astype(BF16).astype(F32)
    lo = (slope2 - hi - mid).astype(BF16).astype(F32)
    q_side = jnp.where(lane == 0, hi, jnp.where(lane == 1, mid, jnp.where(lane == 2, lo, 0.0)))
    return k_side, q_side.astype(BF16)


def _diff_prompt_kernel(slope_ref, lam_ref, subln_ref, q_ref, k_ref, vt_ref, o_ref,
                        acc_ref, sa_ref, sb_ref, *, tile, lam_init):
    i = pl.program_id(2)
    dk = DIFF_HEAD_DIM
    hw = 2 * dk
    gk = KV_GROUP * tile
    last_group = k_ref.shape[0] // gk - 1
    n_heads = slope_ref.shape[0]
    chains = [(h, c) for h in range(n_heads) for c in range(2)]
    krow = lax.broadcasted_iota(jnp.int32, (tile, tile), 0)
    qcol = lax.broadcasted_iota(jnp.int32, (tile, tile), 1)
    slope2 = [slope_ref[h][0:1, 0:1] * LOG2E for h in range(n_heads)]
    sides = [_alibi_operands(slope2[h], tile) for h in range(n_heads)]
    k_side = sides[0][0]
    k_side_g = jnp.concatenate([k_side] * KV_GROUP, axis=0)
    q_aug = {(h, c): jnp.concatenate([q_ref[:, h * hw + c * dk:h * hw + (c + 1) * dk],
                                      sides[h][1]], axis=-1) for h, c in chains}
    acc_ref[...] = jnp.zeros_like(acc_ref)

    def scores(k, side, h, c):
        return _nt_dot(jnp.concatenate([k[:, h * hw + c * dk:h * hw + (c + 1) * dk], side],
                                       axis=-1), q_aug[h, c])

    def produce(s_ref, g):
        k = k_ref[pl.ds(pl.multiple_of(g * gk, gk), gk), :]
        for n, (h, c) in enumerate(chains):
            s = scores(k, k_side_g, h, c)
            for u in range(KV_GROUP):
                s_ref[n, u] = s[u * tile:(u + 1) * tile]

    def update(tiles, get, shifts, state):
        out = []
        for n, (h, c) in enumerate(chains):
            m_prev, l_prev = state[2 * n], state[2 * n + 1]
            m_new = m_prev
            for u, shift in enumerate(shifts[h]):
                m_new = jnp.maximum(m_new, jnp.max(get(n, u), axis=0, keepdims=True) + shift)
            alpha = jnp.exp2(m_prev - m_new)
            l_new = alpha * l_prev
            pv = None
            for u, (t, shift) in enumerate(zip(tiles, shifts[h])):
                p = jnp.exp2(get(n, u) - (m_new - shift))
                l_new = l_new + jnp.sum(p, axis=0, keepdims=True)
                part = jnp.dot(vt_ref[t, h * hw:(h + 1) * hw, :], p.astype(BF16),
                               preferred_element_type=F32)
                pv = part if pv is None else pv + part
            acc_ref[n] = alpha * acc_ref[n] + pv
            out += [m_new, l_new]
        return tuple(out)

    def consume(s_ref, g, state):
        tiles = [g * KV_GROUP + u for u in range(KV_GROUP)]
        shifts = [[jnp.where(t < i, -slope2[h] * ((i - t) * tile).astype(F32), NEG_INF)
                   for t in tiles] for h in range(n_heads)]
        return update(tiles, lambda n, u: s_ref[n, u], shifts, state)

    def body(jj, state):
        produce(sb_ref, 2 * jj + 1)
        state = consume(sa_ref, 2 * jj, state)
        produce(sa_ref, jnp.minimum(2 * jj + 2, last_group))
        return consume(sb_ref, 2 * jj + 1, state)

    neg = jnp.full((1, tile), NEG_INF, F32)
    zero = jnp.zeros((1, tile), F32)
    produce(sa_ref, 0)
    n_pairs = (i + 2 * KV_GROUP - 1) // (2 * KV_GROUP)
    state = lax.fori_loop(0, n_pairs, body, (neg, zero) * len(chains))
    k_diag = k_ref[pl.ds(pl.multiple_of(i * tile, tile), tile), :]
    diag = [jnp.where(qcol >= krow, scores(k_diag, k_side, h, c), NEG_INF) for h, c in chains]
    no_shift = [[jnp.zeros((1, 1), F32)]] * n_heads
    state = update([i], lambda n, u: diag[n], no_shift, state)

    lam = _diff_lambda(lam_ref[...], lam_init)
    for h in range(n_heads):
        l0, l1 = state[4 * h + 1], state[4 * h + 3]
        o = acc_ref[2 * h] * (1.0 / l0) - lam * (acc_ref[2 * h + 1] * (1.0 / l1))
        inv_rms = lax.rsqrt(jnp.mean(o * o, axis=0, keepdims=True) + NORM_EPS)
        o = ((o * inv_rms) * subln_ref[...]) * (1.0 - lam_init)
        o_ref[:, h * hw:(h + 1) * hw] = o.T.astype(o_ref.dtype)


def _diff_prompt(q, k, vt, slopes, lam_params, subln_col, *, batch, seq, heads, lam_init, tile,
                 heads_per_step):
    hw = 2 * DIFF_HEAD_DIM
    nq = seq // tile
    hp = heads_per_step
    kern = functools.partial(_diff_prompt_kernel, tile=tile, lam_init=lam_init)
    return pl.pallas_call(
        kern,
        grid=(batch, heads // hp, nq),
        in_specs=[pl.BlockSpec((hp, SUBLANES, LANES), lambda b, h, i: (h, 0, 0)),
                  pl.BlockSpec((4, DIFF_HEAD_DIM), lambda b, h, i: (0, 0)),
                  pl.BlockSpec((hw, tile), lambda b, h, i: (0, 0)),
                  pl.BlockSpec((tile, hp * hw), lambda b, h, i: (b * nq + i, h)),
                  pl.BlockSpec((seq, hp * hw), lambda b, h, i: (b, h)),
                  pl.BlockSpec((nq, hp * hw, tile), lambda b, h, i: (b, h, 0))],
        out_specs=pl.BlockSpec((tile, hp * hw), lambda b, h, i: (b * nq + i, h)),
        out_shape=jax.ShapeDtypeStruct(q.shape, BF16),
        scratch_shapes=[pltpu.VMEM((2 * hp, hw, tile), F32),
                        pltpu.VMEM((2 * hp, KV_GROUP, tile, tile), F32),
                        pltpu.VMEM((2 * hp, KV_GROUP, tile, tile), F32)],
        compiler_params=_cparams(("parallel", "parallel", "arbitrary")),
        name="diff_prompt",
    )(slopes, lam_params, subln_col, q, k, vt)


def _group_ones(width):
    r = lax.broadcasted_iota(jnp.int32, (width, width), 0) // LANES
    c = lax.broadcasted_iota(jnp.int32, (width, width), 1) // LANES
    return (r == c).astype(BF16)


def _lane_group_sums(x, ones):
    shape = x.shape
    x2 = x.reshape((-1, shape[-1])).astype(BF16)
    return jnp.dot(x2, ones, preferred_element_type=F32).reshape(shape)


def _twice(x):
    return jnp.concatenate([x, x], axis=-1)


def _diff_decode_kernel(pt_ref, slope_ref, lam_ref, subln_ref, q_ref, ks_ref, vs_ref, *rest,
                        pages_per_step, page, past_len, lam_init):
    npg = pages_per_step
    k_refs = rest[:npg]
    v_refs = rest[npg:2 * npg]
    o_ref = rest[2 * npg]
    m_ref, l_ref, acc_ref, tb_ref = rest[2 * npg + 1:]
    j = pl.program_id(1)
    dk = DIFF_HEAD_DIM
    q2 = q_ref[0] * (dk ** -0.5 * LOG2E)
    slope2 = slope_ref[...] * LOG2E
    ones = _group_ones(2 * dk)

    @pl.when(j == 0)
    def _():
        m_ref[...] = jnp.full_like(m_ref, NEG_INF)
        l_ref[...] = jnp.zeros_like(l_ref)
        acc_ref[...] = jnp.zeros_like(acc_ref)
        t_idx = lax.broadcasted_iota(jnp.int32, tb_ref.shape, 0)
        tb_ref[...] = slope2[None] * t_idx.astype(F32)

    def update(ss, shifts, vv):
        m_prev = m_ref[...]
        m_new = m_prev
        for s, shift in zip(ss, shifts):
            m_new = jnp.maximum(m_new, jnp.max(s, axis=0) + shift)
        alpha = jnp.exp2(m_prev - m_new)
        l_new = alpha * l_ref[...]
        acc_new = [_twice(alpha[:, c * dk:(c + 1) * dk]) * acc_ref[c] for c in range(2)]
        for s, shift, v in zip(ss, shifts, vv):
            p = jnp.exp2(s - (m_new - shift)[None])
            l_new = l_new + jnp.sum(p, axis=0)
            for c in range(2):
                pc = _twice(p[:, :, c * dk:(c + 1) * dk])
                acc_new[c] = acc_new[c] + jnp.sum(pc * v, axis=0)
        l_ref[...] = l_new
        for c in range(2):
            acc_ref[c] = acc_new[c]
        m_ref[...] = m_new

    ss, shifts = [], []
    for pg in range(npg):
        base = (j * npg + pg) * page
        shifts.append(-slope2 * (past_len - base).astype(F32))
        ss.append(_lane_group_sums(k_refs[pg][0] * q2[None], ones) + tb_ref[...])
    for pg in range(npg):
        update([ss[pg]], [shifts[pg]], [v_refs[pg][0]])

    @pl.when(j == pl.num_programs(1) - 1)
    def _():
        s_self = _lane_group_sums(ks_ref[0] * q2, ones)
        update([s_self[None]], [jnp.zeros_like(slope2)], [vs_ref[0][None]])
        lam = _diff_lambda(lam_ref[...], lam_init)
        l = l_ref[...]
        o = (acc_ref[0] / _twice(l[:, :dk])) - lam * (acc_ref[1] / _twice(l[:, dk:]))
        o = ((o * _rms_scale(o)) * subln_ref[...]) * (1.0 - lam_init)
        o_ref[0] = o


def _diff_decode(q, ks, vs, cache_k, cache_v, page_table, slopes, lam_params, subln, *,
                 lam_init, pages_per_step):
    b, h, w = q.shape
    page = cache_k.shape[1]
    n_pages = page_table.shape[1]
    npg = pages_per_step
    kern = functools.partial(_diff_decode_kernel, pages_per_step=npg, page=page,
                             past_len=n_pages * page, lam_init=lam_init)

    def cache_spec(pg):
        return pl.BlockSpec((1, page, h, w),
                            lambda bi, j, pt: (pt[bi, j * npg + pg], 0, 0, 0))

    vec = pl.BlockSpec((1, h, w), lambda bi, j, pt: (bi, 0, 0))
    grid_spec = pltpu.PrefetchScalarGridSpec(
        num_scalar_prefetch=1,
        grid=(b, n_pages // npg),
        in_specs=[pl.BlockSpec((h, w), lambda bi, j, pt: (0, 0)),
                  pl.BlockSpec((4, DIFF_HEAD_DIM), lambda bi, j, pt: (0, 0)),
                  pl.BlockSpec((1, w), lambda bi, j, pt: (0, 0)),
                  vec, vec, vec]
                 + [cache_spec(pg) for pg in range(npg)]
                 + [cache_spec(pg) for pg in range(npg)],
        out_specs=vec,
        scratch_shapes=[pltpu.VMEM((h, w), F32),
                        pltpu.VMEM((h, w), F32),
                        pltpu.VMEM((2, h, w), F32),
                        pltpu.VMEM((page, h, w), F32)],
    )
    return pl.pallas_call(
        kern,
        grid_spec=grid_spec,
        out_shape=jax.ShapeDtypeStruct((b, h, w), F32),
        compiler_params=_cparams(("parallel", "arbitrary")),
        name="diff_decode",
    )(page_table, slopes, lam_params, subln, q, ks, vs,
      *([cache_k] * npg), *([cache_v] * npg))


def _moba_kmean_kernel(k_ref, o_ref, *, nblk):
    k = k_ref[...]
    o_ref[0, 0] = jnp.mean(k.reshape(nblk, MOBA_BLOCK, k.shape[-1]), axis=1)


def _moba_kmean(k, *, batch, seq, heads):
    d = MOBA_HEAD_DIM
    nblk = seq // MOBA_BLOCK
    return pl.pallas_call(
        functools.partial(_moba_kmean_kernel, nblk=nblk),
        grid=(batch, heads),
        in_specs=[pl.BlockSpec((seq, d), lambda b, h: (b, h))],
        out_specs=pl.BlockSpec((1, 1, nblk, d), lambda b, h: (b, h, 0, 0)),
        out_shape=jax.ShapeDtypeStruct((batch, heads, nblk, d), F32),
        compiler_params=_cparams(("parallel", "parallel")),
        name="moba_kmean",
    )(k)


def _top_blocks(gate, n_valid, idx, axis):
    valid = idx < n_valid
    g = jnp.where(valid, gate, NEG_INF)
    chosen = jnp.zeros(gate.shape, F32)
    past_end = float(gate.shape[axis])
    for _ in range(MOBA_TOPK):
        mx = jnp.max(g, axis=axis, keepdims=True)
        first = jnp.min(jnp.where(g == mx, idx, past_end), axis=axis, keepdims=True)
        pick = idx == first
        chosen = jnp.where(pick & valid, 1.0, chosen)
        g = jnp.where(pick, -jnp.inf, g)
    return chosen


def _moba_prompt_kernel(slope_ref, km_ref, q_ref, k_ref, vt_ref, o_ref, acc_ref, gb_ref,
                        sa_ref, sb_ref):
    i = pl.program_id(2)
    d = MOBA_HEAD_DIM
    blk = MOBA_BLOCK
    gk = KV_GROUP * blk
    last_group = k_ref.shape[0] // gk - 1
    n_heads = slope_ref.shape[0]
    heads = range(n_heads)
    krow = lax.broadcasted_iota(jnp.int32, (blk, blk), 0)
    qcol = lax.broadcasted_iota(jnp.int32, (blk, blk), 1)
    slope2 = [slope_ref[h][0:1, 0:1] * LOG2E for h in heads]
    sides = [_alibi_operands(slope2[h], blk) for h in heads]
    k_side = sides[0][0]
    k_side_g = jnp.concatenate([k_side] * KV_GROUP, axis=0)
    q = [q_ref[:, h * d:(h + 1) * d] for h in heads]
    q_aug = [jnp.concatenate([q[h], sides[h][1]], axis=-1) for h in heads]

    def scores(k, side, h):
        return _nt_dot(jnp.concatenate([k[:, h * d:(h + 1) * d], side], axis=-1), q_aug[h])

    for h in heads:
        km = km_ref[0, h]
        km_hi = km.astype(BF16)
        km_lo = (km - km_hi.astype(F32)).astype(BF16)
        gate = _nt_dot(km_hi, q[h]) + _nt_dot(km_lo, q[h])
        nrow = lax.broadcasted_iota(jnp.int32, gate.shape, 0).astype(F32)
        chosen = _top_blocks(gate, i.astype(F32), nrow, 0)
        gb_ref[h] = jnp.where(chosen > 0.5, 0.0, NEG_INF)

    k_own = k_ref[pl.ds(pl.multiple_of(i * blk, blk), blk), :]
    state = []
    for h in heads:
        s = jnp.where(qcol >= krow, scores(k_own, k_side, h), NEG_INF)
        m0 = jnp.max(s, axis=0, keepdims=True)
        p = jnp.exp2(s - m0)
        acc_ref[h] = jnp.dot(vt_ref[i, h * d:(h + 1) * d, :], p.astype(BF16),
                             preferred_element_type=F32)
        state += [m0, jnp.sum(p, axis=0, keepdims=True)]

    def produce(s_ref, g):
        k = k_ref[pl.ds(pl.multiple_of(g * gk, gk), gk), :]
        for h in heads:
            s = scores(k, k_side_g, h)
            for u in range(KV_GROUP):
                s_ref[h, u] = s[u * blk:(u + 1) * blk]

    def consume(s_ref, g, state):
        blocks = [g * KV_GROUP + u for u in range(KV_GROUP)]
        out = []
        for h in heads:
            m_prev, l_prev = state[2 * h], state[2 * h + 1]
            shifts = [gb_ref[h, pl.ds(n, 1), :] - slope2[h] * ((i - n) * blk).astype(F32)
                      for n in blocks]
            m_new = m_prev
            for u, shift in enumerate(shifts):
                m_new = jnp.maximum(m_new, jnp.max(s_ref[h, u], axis=0, keepdims=True) + shift)
            alpha = jnp.exp2(m_prev - m_new)
            l_new = alpha * l_prev
            pv = None
            for u, (n, shift) in enumerate(zip(blocks, shifts)):
                p = jnp.exp2(s_ref[h, u] - (m_new - shift))
                l_new = l_new + jnp.sum(p, axis=0, keepdims=True)
                part = jnp.dot(vt_ref[n, h * d:(h + 1) * d, :], p.astype(BF16),
                               preferred_element_type=F32)
                pv = part if pv is None else pv + part
            acc_ref[h] = alpha * acc_ref[h] + pv
            out += [m_new, l_new]
        return tuple(out)

    def body(jj, state):
        produce(sb_ref, 2 * jj + 1)
        state = consume(sa_ref, 2 * jj, state)
        produce(sa_ref, jnp.minimum(2 * jj + 2, last_group))
        return consume(sb_ref, 2 * jj + 1, state)

    produce(sa_ref, 0)
    n_pairs = (i + 2 * KV_GROUP - 1) // (2 * KV_GROUP)
    state = lax.fori_loop(0, n_pairs, body, tuple(state))
    for h in heads:
        o_ref[:, h * d:(h + 1) * d] = (acc_ref[h] * (1.0 / state[2 * h + 1])).T.astype(o_ref.dtype)


def _moba_prompt(q, k, vt, kmean, slopes, *, batch, seq, heads, heads_per_step):
    d = MOBA_HEAD_DIM
    blk = MOBA_BLOCK
    nq = seq // blk
    hp = heads_per_step
    return pl.pallas_call(
        _moba_prompt_kernel,
        grid=(batch, heads // hp, nq),
        in_specs=[pl.BlockSpec((hp, SUBLANES, LANES), lambda b, h, i: (h, 0, 0)),
                  pl.BlockSpec((1, hp, nq, d), lambda b, h, i: (b, h, 0, 0)),
                  pl.BlockSpec((blk, hp * d), lambda b, h, i: (b * nq + i, h)),
                  pl.BlockSpec((seq, hp * d), lambda b, h, i: (b, h)),
                  pl.BlockSpec((nq, hp * d, blk), lambda b, h, i: (b, h, 0))],
        out_specs=pl.BlockSpec((blk, hp * d), lambda b, h, i: (b * nq + i, h)),
        out_shape=jax.ShapeDtypeStruct(q.shape, BF16),
        scratch_shapes=[pltpu.VMEM((hp, d, blk), F32),
                        pltpu.VMEM((hp, nq, blk), F32),
                        pltpu.VMEM((hp, KV_GROUP, blk, blk), F32),
                        pltpu.VMEM((hp, KV_GROUP, blk, blk), F32)],
        compiler_params=_cparams(("parallel", "parallel", "arbitrary")),
        name="moba_prompt",
    )(slopes, kmean, q, k, vt)


def _moba_select_kernel(pt_ref, q_ref, *rest, n_blocks, blocks_per_step, pages_per_block):
    n_in = blocks_per_step * pages_per_block
    k_refs = rest[:n_in]
    sel_ref, gate_ref = rest[n_in:]
    step = pl.program_id(1)
    lane = lax.broadcasted_iota(jnp.int32, gate_ref.shape, 1)

    @pl.when(step == 0)
    def _():
        gate_ref[...] = jnp.full_like(gate_ref, NEG_INF)

    gates = gate_ref[...]
    for u in range(blocks_per_step):
        ksum = jnp.sum(k_refs[u * pages_per_block][0], axis=0)
        for pg in range(1, pages_per_block):
            ksum = ksum + jnp.sum(k_refs[u * pages_per_block + pg][0], axis=0)
        g = jnp.sum(q_ref[0] * (ksum * (1.0 / MOBA_BLOCK)), axis=-1, keepdims=True)
        gates = jnp.where(lane == step * blocks_per_step + u, g, gates)
    gate_ref[...] = gates

    @pl.when(step == pl.num_programs(1) - 1)
    def _():
        lane_f = lane.astype(F32)
        cand = jnp.where(lane < n_blocks, gate_ref[...], NEG_INF)
        sel = jnp.zeros(cand.shape, F32)
        for r in range(MOBA_TOPK):
            mx = jnp.max(cand, axis=-1, keepdims=True)
            first = jnp.min(jnp.where(cand == mx, lane_f, float(LANES)), axis=-1, keepdims=True)
            sel = jnp.where(lane == r, first, sel)
            cand = jnp.where(lane_f == first, -jnp.inf, cand)
        sel_ref[0] = sel.astype(jnp.int32)


def _moba_select(q, cache_k, page_table, *, blocks_per_step):
    b, h, d = q.shape
    page = cache_k.shape[1]
    ppb = MOBA_BLOCK // page
    n_blocks = page_table.shape[1] // ppb
    assert n_blocks <= LANES and n_blocks % blocks_per_step == 0
    n_in = blocks_per_step * ppb

    def kspec(u):
        return pl.BlockSpec((1, page, h, d), lambda bi, n, pt: (pt[bi, n * n_in + u], 0, 0, 0))

    grid_spec = pltpu.PrefetchScalarGridSpec(
        num_scalar_prefetch=1,
        grid=(b, n_blocks // blocks_per_step),
        in_specs=[pl.BlockSpec((1, h, d), lambda bi, n, pt: (bi, 0, 0))]
                 + [kspec(u) for u in range(n_in)],
        out_specs=pl.BlockSpec((1, h, LANES), lambda bi, n, pt: (bi, 0, 0)),
        scratch_shapes=[pltpu.VMEM((h, LANES), F32)],
    )
    return pl.pallas_call(
        functools.partial(_moba_select_kernel, n_blocks=n_blocks,
                          blocks_per_step=blocks_per_step, pages_per_block=ppb),
        grid_spec=grid_spec,
        out_shape=jax.ShapeDtypeStruct((b, h, LANES), jnp.int32),
        compiler_params=_cparams(("parallel", "arbitrary")),
        name="moba_select",
    )(page_table, q, *([cache_k] * n_in))


def _moba_decode_kernel(pg_ref, pos_ref, slope_ref, q_ref, ks_ref, vs_ref, *rest,
                        page, past_len, n_pages):
    k_refs = rest[:n_pages]
    v_refs = rest[n_pages:2 * n_pages]
    o_ref = rest[2 * n_pages]
    m_ref, l_ref, acc_ref = rest[2 * n_pages + 1:]
    bi = pl.program_id(0)
    h = pl.program_id(1)
    hl = h % SUBLANES
    d = MOBA_HEAD_DIM
    scale = d ** -0.5
    q = q_ref[0]
    slope = slope_ref[...]
    ones = _group_ones(d)

    @pl.when(hl == 0)
    def _():
        m_ref[...] = jnp.full_like(m_ref, NEG_INF)
        l_ref[...] = jnp.zeros_like(l_ref)
        acc_ref[...] = jnp.zeros_like(acc_ref)

    def update(ss, vv, live):
        m_prev = m_ref[...]
        m_all = m_prev
        for s in ss:
            m_all = jnp.maximum(m_all, jnp.max(s, axis=0))
        m_new = jnp.where(live, m_all, m_prev)
        alpha = jnp.exp(m_prev - m_new)
        l_new = alpha * l_ref[...]
        acc_new = alpha * acc_ref[...]
        for s, v in zip(ss, vv):
            p = jnp.where(live[None], jnp.exp(s - m_new[None]), 0.0)
            l_new = l_new + jnp.sum(p, axis=0)
            acc_new = acc_new + jnp.sum(p * v, axis=0)
        l_ref[...] = l_new
        acc_ref[...] = acc_new
        m_ref[...] = m_new

    t_idx = lax.broadcasted_iota(jnp.int32, (page, SUBLANES, d), 0)
    ss = []
    for r in range(n_pages):
        kpos = pos_ref[bi, h * n_pages + r] + t_idx
        dist = (past_len - kpos).astype(F32)
        ss.append(_lane_group_sums(k_refs[r][0] * q[None], ones) * scale - slope[None] * dist)
    row = lax.broadcasted_iota(jnp.int32, (SUBLANES, d), 0)
    update(ss, [v_refs[r][0] for r in range(n_pages)], row == hl)

    @pl.when(hl == SUBLANES - 1)
    def _():
        s_self = _lane_group_sums(ks_ref[0] * q, ones) * scale
        update([s_self[None]], [vs_ref[0][None]], row >= 0)
        o_ref[0] = acc_ref[...] / l_ref[...]


def _moba_decode(q, ks, vs, cache_k, cache_v, phys_pages, first_pos, slopes, *, past_len):
    b, h, d = q.shape
    page = cache_k.shape[1]
    n_pages = phys_pages.shape[1] // h

    def cspec(r):
        return pl.BlockSpec((1, page, SUBLANES, d),
                            lambda bi, hh, pg, pos: (pg[bi, hh * n_pages + r], 0, hh // SUBLANES, 0))

    vec = pl.BlockSpec((1, SUBLANES, d), lambda bi, hh, pg, pos: (bi, hh // SUBLANES, 0))
    grid_spec = pltpu.PrefetchScalarGridSpec(
        num_scalar_prefetch=2,
        grid=(b, h),
        in_specs=[pl.BlockSpec((SUBLANES, d), lambda bi, hh, pg, pos: (hh // SUBLANES, 0)),
                  vec, vec, vec]
                 + [cspec(r) for r in range(n_pages)]
                 + [cspec(r) for r in range(n_pages)],
        out_specs=vec,
        scratch_shapes=[pltpu.VMEM((SUBLANES, d), F32),
                        pltpu.VMEM((SUBLANES, d), F32),
                        pltpu.VMEM((SUBLANES, d), F32)],
    )
    return pl.pallas_call(
        functools.partial(_moba_decode_kernel, page=page, past_len=past_len, n_pages=n_pages),
        grid_spec=grid_spec,
        out_shape=jax.ShapeDtypeStruct((b, h, d), F32),
        compiler_params=_cparams(("parallel", "arbitrary")),
        name="moba_decode",
    )(phys_pages, first_pos, slopes, q, ks, vs,
      *([cache_k] * n_pages), *([cache_v] * n_pages))


def _alibi_slopes(n_heads):
    return jnp.asarray([2.0 ** (-8.0 * (h + 1) / n_heads) for h in range(n_heads)], F32)


def _lambda_init_value(layer_idx):
    return 0.8 - 0.6 * math.exp(-0.3 * layer_idx)


def kernel(x_prompt, x_sample, cache_k_l0, cache_v_l0, cache_k_l1, cache_v_l1, page_table,
           w_qkv, w_o, w_gate_up, w_down, g_mix_pre, g_mix_post, g_ffn_pre, g_ffn_post,
           diff_lambda, diff_subln):
    bp, sp, dm = x_prompt.shape
    bs, ds, _ = x_sample.shape
    assert ds == 1
    depth = w_qkv.shape[0]
    page = cache_k_l0.shape[1]
    past_len = page_table.shape[1] * page
    diff_heads = cache_k_l0.shape[2]
    moba_heads = cache_k_l1.shape[2]

    hp = x_prompt.reshape(bp * sp, dm)
    hs = x_sample.reshape(bs * ds, dm)
    wqkv_b = w_qkv.astype(BF16)
    wo_b = w_o.astype(BF16)
    wgu_b = w_gate_up.astype(BF16)
    wdn_b = w_down.astype(BF16)

    rows = []
    for i in range(depth):
        g_pre = g_mix_pre[i].reshape(1, dm)
        head_dim = DIFF_HEAD_DIM if i % 2 == 0 else MOBA_HEAD_DIM
        qp, kp, vp, kpb, vpt = _norm_qkv(hp, g_pre, wqkv_b, i, tm=1024, tn=512, q_dtype=BF16,
                                         tile=ATTN_TILE, q_scale=head_dim ** -0.5 * LOG2E)
        qs, ks, vs = _norm_qkv(hs, g_pre, w_qkv, i, tm=bs * ds, tn=512, q_dtype=F32)
        if i % 2 == 0:
            jd = i // 2
            lam_init = _lambda_init_value(i)
            hw = 2 * DIFF_HEAD_DIM
            sl = _alibi_slopes(diff_heads)
            slopes_p = jnp.broadcast_to(sl[:, None, None], (diff_heads, SUBLANES, LANES))
            slopes_s = jnp.broadcast_to(sl[:, None], (diff_heads, hw))
            subln = diff_subln[jd].reshape(1, hw)
            subln_col = jnp.broadcast_to(diff_subln[jd][:, None], (hw, ATTN_TILE))
            op = _diff_prompt(qp, kpb, vpt, slopes_p, diff_lambda[jd], subln_col,
                              batch=bp, seq=sp, heads=diff_heads, lam_init=lam_init,
                              tile=ATTN_TILE, heads_per_step=2)
            os_ = _diff_decode(qs.reshape(bs, diff_heads, hw), ks.reshape(bs, diff_heads, hw),
                               vs.reshape(bs, diff_heads, hw), cache_k_l0, cache_v_l0,
                               page_table, slopes_s, diff_lambda[jd], subln,
                               lam_init=lam_init, pages_per_step=8)
            os_ = os_.reshape(bs * ds, dm)
            rows.append((kp.reshape(bp, sp, diff_heads, hw), vp.reshape(bp, sp, diff_heads, hw),
                         ks.reshape(bs, ds, diff_heads, hw), vs.reshape(bs, ds, diff_heads, hw)))
        else:
            d = MOBA_HEAD_DIM
            sl = _alibi_slopes(moba_heads)
            slopes_p = jnp.broadcast_to(sl[:, None, None], (moba_heads, SUBLANES, LANES))
            slopes_s = jnp.broadcast_to(sl[:, None], (moba_heads, d))
            kmean = _moba_kmean(kp, batch=bp, seq=sp, heads=moba_heads)
            op = _moba_prompt(qp, kpb, vpt, kmean, slopes_p, batch=bp, seq=sp, heads=moba_heads,
                              heads_per_step=4)
            q3 = qs.reshape(bs, moba_heads, d)
            k3 = ks.reshape(bs, moba_heads, d)
            v3 = vs.reshape(bs, moba_heads, d)
            sel = _moba_select(q3, cache_k_l1, page_table,
                               blocks_per_step=2)[:, :, :MOBA_TOPK]
            ppb = MOBA_BLOCK // page
            logical = (sel[..., None] * ppb + jnp.arange(ppb, dtype=jnp.int32)).reshape(bs, -1)
            phys = jnp.take_along_axis(page_table, logical, axis=1)
            os_ = _moba_decode(q3, k3, v3, cache_k_l1, cache_v_l1, phys, logical * page,
                               slopes_s, past_len=past_len)
            os_ = os_.reshape(bs * ds, dm)
            rows.append((kp.reshape(bp, sp, moba_heads, d), vp.reshape(bp, sp, moba_heads, d),
                         ks.reshape(bs, ds, moba_heads, d), vs.reshape(bs, ds, moba_heads, d)))

        g_post = g_mix_post[i].reshape(1, dm)
        hp = _mm_norm_res(op, wo_b, i, g_post, hp, tm=512)
        hs = _mm_norm_res(os_, w_o, i, g_post, hs, tm=bs * ds, tk=512)
        g_fpre = g_ffn_pre[i].reshape(1, dm)
        g_fpost = g_ffn_post[i].reshape(1, dm)
        fp = _norm_swiglu(hp, g_fpre, wgu_b, i, tm=1024, tn=512)
        hp = _mm_norm_res(fp, wdn_b, i, g_fpost, hp, tm=512)
        fs = _norm_swiglu(hs, g_fpre, w_gate_up, i, tm=bs * ds, tn=512)
        hs = _mm_norm_res(fs, w_down, i, g_fpost, hs, tm=bs * ds, tk=512)

    (k0p, v0p, k0s, v0s), (k1p, v1p, k1s, v1s) = rows
    return (hp.reshape(bp, sp, dm), hs.reshape(bs, ds, dm),
            k0p, v0p, k0s, v0s, k1p, v1p, k1s, v1s)
```

```python
import functools
import math

import jax
import jax.numpy as jnp
from jax import lax
from jax.experimental import pallas as pl
from jax.experimental.pallas import tpu as pltpu

F32 = jnp.float32
BF16 = jnp.bfloat16

NORM_EPS = 1e-6
NEG_INF = -1e30
LOG2E = math.log2(math.e)
DIFF_HEAD_DIM = 128
MOBA_HEAD_DIM = 128
MOBA_BLOCK = 256
MOBA_TOPK = 3
ATTN_TILE = MOBA_BLOCK
KV_GROUP = 2

LANES = 128
SUBLANES = 8
VMEM_LIMIT = 56 * 1024 * 1024


def _cparams(sem):
    return pltpu.CompilerParams(dimension_semantics=sem, vmem_limit_bytes=VMEM_LIMIT)


def _nt_dot(a, b):
    return lax.dot_general(a, b, (((1,), (1,)), ((), ())), preferred_element_type=F32)


def _mm(a, w):
    precision = lax.Precision.HIGHEST if w.dtype == F32 else None
    return jnp.dot(a.astype(w.dtype), w, preferred_element_type=F32, precision=precision)


def _rms_scale(x):
    return lax.rsqrt(jnp.mean(x * x, axis=-1, keepdims=True) + NORM_EPS)


def _norm_qkv_kernel(x_ref, g_ref, wq_ref, wk_ref, wv_ref, q_ref, k_ref, v_ref, *rest,
                     tile, q_scale):
    xn_ref = rest[-1]

    @pl.when(pl.program_id(1) == 0)
    def _():
        x = x_ref[...]
        xn_ref[...] = ((x * _rms_scale(x)) * g_ref[...]).astype(xn_ref.dtype)

    xn = xn_ref[...]
    q = _mm(xn, wq_ref[...])
    k = _mm(xn, wk_ref[...])
    v = _mm(xn, wv_ref[...])
    q_ref[...] = (q if q_scale is None else q * q_scale).astype(q_ref.dtype)
    k_ref[...] = k
    v_ref[...] = v
    if len(rest) == 3:
        kb_ref, vt_ref = rest[:2]
        kb_ref[...] = k.astype(BF16)
        for u in range(vt_ref.shape[0]):
            vt_ref[u] = v[u * tile:(u + 1) * tile, :].T.astype(BF16)


def _norm_qkv(x, g, w, layer, *, tm, tn, q_dtype, tile=None, q_scale=None):
    m, d = x.shape
    nj = d // tn
    wspec = lambda off: pl.BlockSpec((None, d, tn), lambda i, j: (layer, 0, j + off))
    ospec = pl.BlockSpec((tm, tn), lambda i, j: (i, j))
    out_specs = [ospec] * 3
    out_shape = [jax.ShapeDtypeStruct((m, d), q_dtype),
                 jax.ShapeDtypeStruct((m, d), F32),
                 jax.ShapeDtypeStruct((m, d), F32)]
    if tile is not None:
        out_specs += [ospec, pl.BlockSpec((tm // tile, tn, tile), lambda i, j: (i, j, 0))]
        out_shape += [jax.ShapeDtypeStruct((m, d), BF16),
                      jax.ShapeDtypeStruct((m // tile, d, tile), BF16)]
    return pl.pallas_call(
        functools.partial(_norm_qkv_kernel, tile=tile, q_scale=q_scale),
        grid=(m // tm, nj),
        in_specs=[pl.BlockSpec((tm, d), lambda i, j: (i, 0)),
                  pl.BlockSpec((1, d), lambda i, j: (0, 0)),
                  wspec(0), wspec(nj), wspec(2 * nj)],
        out_specs=out_specs,
        out_shape=out_shape,
        scratch_shapes=[pltpu.VMEM((tm, d), w.dtype)],
        compiler_params=_cparams(("parallel", "arbitrary")),
        name="norm_qkv",
    )(x, g, w, w, w)


def _norm_swiglu_kernel(x_ref, g_ref, wg_ref, wu_ref, h_ref, xn_ref):
    @pl.when(pl.program_id(1) == 0)
    def _():
        x = x_ref[...]
        xn_ref[...] = ((x * _rms_scale(x)) * g_ref[...]).astype(xn_ref.dtype)

    xn = xn_ref[...]
    gate = _mm(xn, wg_ref[...])
    up = _mm(xn, wu_ref[...])
    h_ref[...] = ((gate * (1.0 / (1.0 + jnp.exp(-gate)))) * up).astype(h_ref.dtype)


def _norm_swiglu(x, g, w, layer, *, tm, tn):
    m, d = x.shape
    f = w.shape[2] // 2
    nj = f // tn
    return pl.pallas_call(
        _norm_swiglu_kernel,
        grid=(m // tm, nj),
        in_specs=[pl.BlockSpec((tm, d), lambda i, j: (i, 0)),
                  pl.BlockSpec((1, d), lambda i, j: (0, 0)),
                  pl.BlockSpec((None, d, tn), lambda i, j: (layer, 0, j)),
                  pl.BlockSpec((None, d, tn), lambda i, j: (layer, 0, j + nj))],
        out_specs=pl.BlockSpec((tm, tn), lambda i, j: (i, j)),
        out_shape=jax.ShapeDtypeStruct((m, f), w.dtype),
        scratch_shapes=[pltpu.VMEM((tm, d), w.dtype)],
        compiler_params=_cparams(("parallel", "arbitrary")),
        name="norm_swiglu",
    )(x, g, w, w)


def _mm_norm_res_kernel(a_ref, w_ref, g_ref, r_ref, o_ref, *acc):
    def finish(y):
        o_ref[...] = r_ref[...] + (y * _rms_scale(y)) * g_ref[...]

    if not acc:
        finish(_mm(a_ref[...], w_ref[...]))
        return

    acc_ref, = acc
    kk = pl.program_id(1)

    @pl.when(kk == 0)
    def _():
        acc_ref[...] = jnp.zeros_like(acc_ref)

    acc_ref[...] += _mm(a_ref[...], w_ref[...])

    @pl.when(kk == pl.num_programs(1) - 1)
    def _():
        finish(acc_ref[...])


def _mm_norm_res(a, w, layer, g, res, *, tm, tk=None):
    m, kdim = a.shape
    n = w.shape[2]
    if tk is None:
        tk = kdim
        w_spec = pl.BlockSpec((None, kdim, n), lambda i, k: (layer, 0, 0),
                              pipeline_mode=pl.Buffered(1))
        scratch = []
    else:
        w_spec = pl.BlockSpec((None, tk, n), lambda i, k: (layer, k, 0))
        scratch = [pltpu.VMEM((tm, n), F32)]
    return pl.pallas_call(
        _mm_norm_res_kernel,
        grid=(m // tm, kdim // tk),
        in_specs=[pl.BlockSpec((tm, tk), lambda i, k: (i, k)),
                  w_spec,
                  pl.BlockSpec((1, n), lambda i, k: (0, 0)),
                  pl.BlockSpec((tm, n), lambda i, k: (i, 0))],
        out_specs=pl.BlockSpec((tm, n), lambda i, k: (i, 0)),
        out_shape=jax.ShapeDtypeStruct((m, n), F32),
        scratch_shapes=scratch,
        compiler_params=_cparams(("parallel", "arbitrary")),
        name="mm_norm_res",
    )(a, w, g, res)


def _diff_lambda(lp, lam_init):
    a = jnp.sum(lp[0:1] * lp[1:2], axis=-1, keepdims=True)
    b = jnp.sum(lp[2:3] * lp[3:4], axis=-1, keepdims=True)
    return jnp.exp(a) - jnp.exp(b) + lam_init


def _alibi_operands(slope2, tile):
    assert tile <= 256
    lane = lax.broadcasted_iota(jnp.int32, (tile, LANES), 1)
    row = lax.broadcasted_iota(jnp.int32, (tile, LANES), 0).astype(F32)
    k_side = jnp.where(lane < 3, row, 0.0).astype(BF16)
    hi = slope2.astype(BF16).astype(F32)
    mid = (slope2 - hi).astype(BF16).astype(F32)
    lo = (slope2 - hi - mid).astype(BF16).astype(F32)
    q_side = jnp.where(lane == 0, hi, jnp.where(lane == 1, mid, jnp.where(lane == 2, lo, 0.0)))
    return k_side, q_side.astype(BF16)


def _diff_prompt_kernel(slope_ref, lam_ref, subln_ref, q_ref, k_ref, vt_ref, o_ref,
                        acc_ref, sa_ref, sb_ref, *, tile, lam_init):
    i = pl.program_id(2)
    dk = DIFF_HEAD_DIM
    hw = 2 * dk
    gk = KV_GROUP * tile
    last_group = k_ref.shape[0] // gk - 1
    n_heads = slope_ref.shape[0]
    chains = [(h, c) for h in range(n_heads) for c in range(2)]
    krow = lax.broadcasted_iota(jnp.int32, (tile, tile), 0)
    qcol = lax.broadcasted_iota(jnp.int32, (tile, tile), 1)
    slope2 = [slope_ref[h][0:1, 0:1] * LOG2E for h in range(n_heads)]
    sides = [_alibi_operands(slope2[h], tile) for h in range(n_heads)]
    k_side = sides[0][0]
    k_side_g = jnp.concatenate([k_side] * KV_GROUP, axis=0)
    q_aug = {(h, c): jnp.concatenate([q_ref[:, h * hw + c * dk:h * hw + (c + 1) * dk],
                                      sides[h][1]], axis=-1) for h, c in chains}
    acc_ref[...] = jnp.zeros_like(acc_ref)

    def scores(k, side, h, c):
        return _nt_dot(jnp.concatenate([k[:, h * hw + c * dk:h * hw + (c + 1) * dk], side],
                                       axis=-1), q_aug[h, c])

    def produce(s_ref, g):
        k = k_ref[pl.ds(pl.multiple_of(g * gk, gk), gk), :]
        for n, (h, c) in enumerate(chains):
            s = scores(k, k_side_g, h, c)
            for u in range(KV_GROUP):
                s_ref[n, u] = s[u * tile:(u + 1) * tile]

    def update(tiles, get, shifts, state):
        out = []
        for n, (h, c) in enumerate(chains):
            m_prev, l_prev = state[2 * n], state[2 * n + 1]
            m_new = m_prev
            for u, shift in enumerate(shifts[h]):
                m_new = jnp.maximum(m_new, jnp.max(get(n, u), axis=0, keepdims=True) + shift)
            alpha = jnp.exp2(m_prev - m_new)
            l_new = alpha * l_prev
            pv = None
            for u, (t, shift) in enumerate(zip(tiles, shifts[h])):
                p = jnp.exp2(get(n, u) - (m_new - shift))
                l_new = l_new + jnp.sum(p, axis=0, keepdims=True)
                part = jnp.dot(vt_ref[t, h * hw:(h + 1) * hw, :], p.astype(BF16),
                               preferred_element_type=F32)
                pv = part if pv is None else pv + part
            acc_ref[n] = alpha * acc_ref[n] + pv
            out += [m_new, l_new]
        return tuple(out)

    def consume(s_ref, g, state):
        tiles = [g * KV_GROUP + u for u in range(KV_GROUP)]
        shifts = [[jnp.where(t < i, -slope2[h] * ((i - t) * tile).astype(F32), NEG_INF)
                   for t in tiles] for h in range(n_heads)]
        return update(tiles, lambda n, u: s_ref[n, u], shifts, state)

    def body(jj, state):
        produce(sb_ref, 2 * jj + 1)
        state = consume(sa_ref, 2 * jj, state)
        produce(sa_ref, jnp.minimum(2 * jj + 2, last_group))
        return consume(sb_ref, 2 * jj + 1, state)

    neg = jnp.full((1, tile), NEG_INF, F32)
    zero = jnp.zeros((1, tile), F32)
    produce(sa_ref, 0)
    n_pairs = (i + 2 * KV_GROUP - 1) // (2 * KV_GROUP)
    state = lax.fori_loop(0, n_pairs, body, (neg, zero) * len(chains))
    k_diag = k_ref[pl.ds(pl.multiple_of(i * tile, tile), tile), :]
    diag = [jnp.where(qcol >= krow, scores(k_diag, k_side, h, c), NEG_INF) for h, c in chains]
    no_shift = [[jnp.zeros((1, 1), F32)]] * n_heads
    state = update([i], lambda n, u: diag[n], no_shift, state)

    lam = _diff_lambda(lam_ref[...], lam_init)
    for h in range(n_heads):
        l0, l1 = state[4 * h + 1], state[4 * h + 3]
        o = acc_ref[2 * h] * (1.0 / l0) - lam * (acc_ref[2 * h + 1] * (1.0 / l1))
        inv_rms = lax.rsqrt(jnp.mean(o * o, axis=0, keepdims=True) + NORM_EPS)
        o = ((o * inv_rms) * subln_ref[...]) * (1.0 - lam_init)
        o_ref[:, h * hw:(h + 1) * hw] = o.T.astype(o_ref.dtype)


def _diff_prompt(q, k, vt, slopes, lam_params, subln_col, *, batch, seq, heads, lam_init, tile,
                 heads_per_step):
    hw = 2 * DIFF_HEAD_DIM
    nq = seq // tile
    hp = heads_per_step
    kern = functools.partial(_diff_prompt_kernel, tile=tile, lam_init=lam_init)
    return pl.pallas_call(
        kern,
        grid=(batch, heads // hp, nq),
        in_specs=[pl.BlockSpec((hp, SUBLANES, LANES), lambda b, h, i: (h, 0, 0)),
                  pl.BlockSpec((4, DIFF_HEAD_DIM), lambda b, h, i: (0, 0)),
                  pl.BlockSpec((hw, tile), lambda b, h, i: (0, 0)),
                  pl.BlockSpec((tile, hp * hw), lambda b, h, i: (b * nq + i, h)),
                  pl.BlockSpec((seq, hp * hw), lambda b, h, i: (b, h)),
                  pl.BlockSpec((nq, hp * hw, tile), lambda b, h, i: (b, h, 0))],
        out_specs=pl.BlockSpec((tile, hp * hw), lambda b, h, i: (b * nq + i, h)),
        out_shape=jax.ShapeDtypeStruct(q.shape, BF16),
        scratch_shapes=[pltpu.VMEM((2 * hp, hw, tile), F32),
                        pltpu.VMEM((2 * hp, KV_GROUP, tile, tile), F32),
                        pltpu.VMEM((2 * hp, KV_GROUP, tile, tile), F32)],
        compiler_params=_cparams(("parallel", "parallel", "arbitrary")),
        name="diff_prompt",
    )(slopes, lam_params, subln_col, q, k, vt)


def _group_ones(width):
    r = lax.broadcasted_iota(jnp.int32, (width, width), 0) // LANES
    c = lax.broadcasted_iota(jnp.int32, (width, width), 1) // LANES
    return (r == c).astype(BF16)


def _lane_group_sums(x, ones):
    shape = x.shape
    x2 = x.reshape((-1, shape[-1])).astype(BF16)
    return jnp.dot(x2, ones, preferred_element_type=F32).reshape(shape)


def _twice(x):
    return jnp.concatenate([x, x], axis=-1)


def _diff_decode_kernel(pt_ref, slope_ref, lam_ref, subln_ref, q_ref, ks_ref, vs_ref, *rest,
                        pages_per_step, page, past_len, lam_init):
    npg = pages_per_step
    k_refs = rest[:npg]
    v_refs = rest[npg:2 * npg]
    o_ref = rest[2 * npg]
    m_ref, l_ref, acc_ref, tb_ref = rest[2 * npg + 1:]
    j = pl.program_id(1)
    dk = DIFF_HEAD_DIM
    q2 = q_ref[0] * (dk ** -0.5 * LOG2E)
    slope2 = slope_ref[...] * LOG2E
    ones = _group_ones(2 * dk)

    @pl.when(j == 0)
    def _():
        m_ref[...] = jnp.full_like(m_ref, NEG_INF)
        l_ref[...] = jnp.zeros_like(l_ref)
        acc_ref[...] = jnp.zeros_like(acc_ref)
        t_idx = lax.broadcasted_iota(jnp.int32, tb_ref.shape, 0)
        tb_ref[...] = slope2[None] * t_idx.astype(F32)

    def update(ss, shifts, vv):
        m_prev = m_ref[...]
        m_new = m_prev
        for s, shift in zip(ss, shifts):
            m_new = jnp.maximum(m_new, jnp.max(s, axis=0) + shift)
        alpha = jnp.exp2(m_prev - m_new)
        l_new = alpha * l_ref[...]
        acc_new = [_twice(alpha[:, c * dk:(c + 1) * dk]) * acc_ref[c] for c in range(2)]
        for s, shift, v in zip(ss, shifts, vv):
            p = jnp.exp2(s - (m_new - shift)[None])
            l_new = l_new + jnp.sum(p, axis=0)
            for c in range(2):
                pc = _twice(p[:, :, c * dk:(c + 1) * dk])
                acc_new[c] = acc_new[c] + jnp.sum(pc * v, axis=0)
        l_ref[...] = l_new
        for c in range(2):
            acc_ref[c] = acc_new[c]
        m_ref[...] = m_new

    ss, shifts = [], []
    for pg in range(npg):
        base = (j * npg + pg) * page
        shifts.append(-slope2 * (past_len - base).astype(F32))
        ss.append(_lane_group_sums(k_refs[pg][0] * q2[None], ones) + tb_ref[...])
    for pg in range(npg):
        update([ss[pg]], [shifts[pg]], [v_refs[pg][0]])

    @pl.when(j == pl.num_programs(1) - 1)
    def _():
        s_self = _lane_group_sums(ks_ref[0] * q2, ones)
        update([s_self[None]], [jnp.zeros_like(slope2)], [vs_ref[0][None]])
        lam = _diff_lambda(lam_ref[...], lam_init)
        l = l_ref[...]
        o = (acc_ref[0] / _twice(l[:, :dk])) - lam * (acc_ref[1] / _twice(l[:, dk:]))
        o = ((o * _rms_scale(o)) * subln_ref[...]) * (1.0 - lam_init)
        o_ref[0] = o


def _diff_decode(q, ks, vs, cache_k, cache_v, page_table, slopes, lam_params, subln, *,
                 lam_init, pages_per_step):
    b, h, w = q.shape
    page = cache_k.shape[1]
    n_pages = page_table.shape[1]
    npg = pages_per_step
    kern = functools.partial(_diff_decode_kernel, pages_per_step=npg, page=page,
                             past_len=n_pages * page, lam_init=lam_init)

    def cache_spec(pg):
        return pl.BlockSpec((1, page, h, w),
                            lambda bi, j, pt: (pt[bi, j * npg + pg], 0, 0, 0))

    vec = pl.BlockSpec((1, h, w), lambda bi, j, pt: (bi, 0, 0))
    grid_spec = pltpu.PrefetchScalarGridSpec(
        num_scalar_prefetch=1,
        grid=(b, n_pages // npg),
        in_specs=[pl.BlockSpec((h, w), lambda bi, j, pt: (0, 0)),
                  pl.BlockSpec((4, DIFF_HEAD_DIM), lambda bi, j, pt: (0, 0)),
                  pl.BlockSpec((1, w), lambda bi, j, pt: (0, 0)),
                  vec, vec, vec]
                 + [cache_spec(pg) for pg in range(npg)]
                 + [cache_spec(pg) for pg in range(npg)],
        out_specs=vec,
        scratch_shapes=[pltpu.VMEM((h, w), F32),
                        pltpu.VMEM((h, w), F32),
                        pltpu.VMEM((2, h, w), F32),
                        pltpu.VMEM((page, h, w), F32)],
    )
    return pl.pallas_call(
        kern,
        grid_spec=grid_spec,
        out_shape=jax.ShapeDtypeStruct((b, h, w), F32),
        compiler_params=_cparams(("parallel", "arbitrary")),
        name="diff_decode",
    )(page_table, slopes, lam_params, subln, q, ks, vs,
      *([cache_k] * npg), *([cache_v] * npg))


def _top_blocks(gate, n_valid, idx, axis):
    valid = idx < n_valid
    g = jnp.where(valid, gate, NEG_INF)
    chosen = jnp.zeros(gate.shape, F32)
    past_end = float(gate.shape[axis])
    for _ in range(MOBA_TOPK):
        mx = jnp.max(g, axis=axis, keepdims=True)
        first = jnp.min(jnp.where(g == mx, idx, past_end), axis=axis, keepdims=True)
        pick = idx == first
        chosen = jnp.where(pick & valid, 1.0, chosen)
        g = jnp.where(pick, -jnp.inf, g)
    return chosen


def _moba_prompt_kernel(slope_ref, q_ref, k_ref, vt_ref, o_ref, acc_ref, gb_ref, km_ref,
                        sa_ref, sb_ref):
    i = pl.program_id(2)
    d = MOBA_HEAD_DIM
    blk = MOBA_BLOCK
    gk = KV_GROUP * blk
    last_group = k_ref.shape[0] // gk - 1
    n_heads = slope_ref.shape[0]
    heads = range(n_heads)
    krow = lax.broadcasted_iota(jnp.int32, (blk, blk), 0)
    qcol = lax.broadcasted_iota(jnp.int32, (blk, blk), 1)
    slope2 = [slope_ref[h][0:1, 0:1] * LOG2E for h in heads]
    sides = [_alibi_operands(slope2[h], blk) for h in heads]
    k_side = sides[0][0]
    k_side_g = jnp.concatenate([k_side] * KV_GROUP, axis=0)
    q = [q_ref[:, h * d:(h + 1) * d] for h in heads]
    q_aug = [jnp.concatenate([q[h], sides[h][1]], axis=-1) for h in heads]

    def scores(k, side, h):
        return _nt_dot(jnp.concatenate([k[:, h * d:(h + 1) * d], side], axis=-1), q_aug[h])

    @pl.when(i == 0)
    def _():
        for h in heads:
            k_all = k_ref[:, h * d:(h + 1) * d].astype(F32)
            km_ref[h] = jnp.mean(k_all.reshape(-1, blk, d), axis=1)

    for h in heads:
        km = km_ref[h]
        km_hi = km.astype(BF16)
        km_lo = (km - km_hi.astype(F32)).astype(BF16)
        gate = _nt_dot(km_hi, q[h]) + _nt_dot(km_lo, q[h])
        nrow = lax.broadcasted_iota(jnp.int32, gate.shape, 0).astype(F32)
        chosen = _top_blocks(gate, i.astype(F32), nrow, 0)
        gb_ref[h] = jnp.where(chosen > 0.5, 0.0, NEG_INF)

    k_own = k_ref[pl.ds(pl.multiple_of(i * blk, blk), blk), :]
    state = []
    for h in heads:
        s = jnp.where(qcol >= krow, scores(k_own, k_side, h), NEG_INF)
        m0 = jnp.max(s, axis=0, keepdims=True)
        p = jnp.exp2(s - m0)
        acc_ref[h] = jnp.dot(vt_ref[i, h * d:(h + 1) * d, :], p.astype(BF16),
                             preferred_element_type=F32)
        state += [m0, jnp.sum(p, axis=0, keepdims=True)]

    def produce(s_ref, g):
        k = k_ref[pl.ds(pl.multiple_of(g * gk, gk), gk), :]
        for h in heads:
            s = scores(k, k_side_g, h)
            for u in range(KV_GROUP):
                s_ref[h, u] = s[u * blk:(u + 1) * blk]

    def consume(s_ref, g, state):
        blocks = [g * KV_GROUP + u for u in range(KV_GROUP)]
        out = []
        for h in heads:
            m_prev, l_prev = state[2 * h], state[2 * h + 1]
            shifts = [gb_ref[h, pl.ds(n, 1), :] - slope2[h] * ((i - n) * blk).astype(F32)
                      for n in blocks]
            m_new = m_prev
            for u, shift in enumerate(shifts):
                m_new = jnp.maximum(m_new, jnp.max(s_ref[h, u], axis=0, keepdims=True) + shift)
            alpha = jnp.exp2(m_prev - m_new)
            l_new = alpha * l_prev
            pv = None
            for u, (n, shift) in enumerate(zip(blocks, shifts)):
                p = jnp.exp2(s_ref[h, u] - (m_new - shift))
                l_new = l_new + jnp.sum(p, axis=0, keepdims=True)
                part = jnp.dot(vt_ref[n, h * d:(h + 1) * d, :], p.astype(BF16),
                               preferred_element_type=F32)
                pv = part if pv is None else pv + part
            acc_ref[h] = alpha * acc_ref[h] + pv
            out += [m_new, l_new]
        return tuple(out)

    def body(jj, state):
        produce(sb_ref, 2 * jj + 1)
        state = consume(sa_ref, 2 * jj, state)
        produce(sa_ref, jnp.minimum(2 * jj + 2, last_group))
        return consume(sb_ref, 2 * jj + 1, state)

    produce(sa_ref, 0)
    n_pairs = (i + 2 * KV_GROUP - 1) // (2 * KV_GROUP)
    state = lax.fori_loop(0, n_pairs, body, tuple(state))
    for h in heads:
        o_ref[:, h * d:(h + 1) * d] = (acc_ref[h] * (1.0 / state[2 * h + 1])).T.astype(o_ref.dtype)


def _moba_prompt(q, k, vt, slopes, *, batch, seq, heads, heads_per_step):
    d = MOBA_HEAD_DIM
    blk = MOBA_BLOCK
    nq = seq // blk
    hp = heads_per_step
    return pl.pallas_call(
        _moba_prompt_kernel,
        grid=(batch, heads // hp, nq),
        in_specs=[pl.BlockSpec((hp, SUBLANES, LANES), lambda b, h, i: (h, 0, 0)),
                  pl.BlockSpec((blk, hp * d), lambda b, h, i: (b * nq + i, h)),
                  pl.BlockSpec((seq, hp * d), lambda b, h, i: (b, h)),
                  pl.BlockSpec((nq, hp * d, blk), lambda b, h, i: (b, h, 0))],
        out_specs=pl.BlockSpec((blk, hp * d), lambda b, h, i: (b * nq + i, h)),
        out_shape=jax.ShapeDtypeStruct(q.shape, BF16),
        scratch_shapes=[pltpu.VMEM((hp, d, blk), F32),
                        pltpu.VMEM((hp, nq, blk), F32),
                        pltpu.VMEM((hp, nq, d), F32),
                        pltpu.VMEM((hp, KV_GROUP, blk, blk), F32),
                        pltpu.VMEM((hp, KV_GROUP, blk, blk), F32)],
        compiler_params=_cparams(("parallel", "parallel", "arbitrary")),
        name="moba_prompt",
    )(slopes, q, k, vt)


def _moba_select_kernel(pt_ref, q_ref, *rest, n_blocks, blocks_per_step, pages_per_block):
    n_in = blocks_per_step * pages_per_block
    k_refs = rest[:n_in]
    sel_ref, gate_ref = rest[n_in:]
    step = pl.program_id(1)
    lane = lax.broadcasted_iota(jnp.int32, gate_ref.shape, 1)

    @pl.when(step == 0)
    def _():
        gate_ref[...] = jnp.full_like(gate_ref, NEG_INF)

    gates = gate_ref[...]
    for u in range(blocks_per_step):
        ksum = jnp.sum(k_refs[u * pages_per_block][0], axis=0)
        for pg in range(1, pages_per_block):
            ksum = ksum + jnp.sum(k_refs[u * pages_per_block + pg][0], axis=0)
        g = jnp.sum(q_ref[0] * (ksum * (1.0 / MOBA_BLOCK)), axis=-1, keepdims=True)
        gates = jnp.where(lane == step * blocks_per_step + u, g, gates)
    gate_ref[...] = gates

    @pl.when(step == pl.num_programs(1) - 1)
    def _():
        lane_f = lane.astype(F32)
        cand = jnp.where(lane < n_blocks, gate_ref[...], NEG_INF)
        sel = jnp.zeros(cand.shape, F32)
        for r in range(MOBA_TOPK):
            mx = jnp.max(cand, axis=-1, keepdims=True)
            first = jnp.min(jnp.where(cand == mx, lane_f, float(LANES)), axis=-1, keepdims=True)
            sel = jnp.where(lane == r, first, sel)
            cand = jnp.where(lane_f == first, -jnp.inf, cand)
        sel_ref[0] = sel.astype(jnp.int32)


def _moba_select(q, cache_k, page_table, *, blocks_per_step):
    b, h, d = q.shape
    page = cache_k.shape[1]
    ppb = MOBA_BLOCK // page
    n_blocks = page_table.shape[1] // ppb
    assert n_blocks <= LANES and n_blocks % blocks_per_step == 0
    n_in = blocks_per_step * ppb

    def kspec(u):
        return pl.BlockSpec((1, page, h, d), lambda bi, n, pt: (pt[bi, n * n_in + u], 0, 0, 0))

    grid_spec = pltpu.PrefetchScalarGridSpec(
        num_scalar_prefetch=1,
        grid=(b, n_blocks // blocks_per_step),
        in_specs=[pl.BlockSpec((1, h, d), lambda bi, n, pt: (bi, 0, 0))]
                 + [kspec(u) for u in range(n_in)],
        out_specs=pl.BlockSpec((1, h, LANES), lambda bi, n, pt: (bi, 0, 0)),
        scratch_shapes=[pltpu.VMEM((h, LANES), F32)],
    )
    return pl.pallas_call(
        functools.partial(_moba_select_kernel, n_blocks=n_blocks,
                          blocks_per_step=blocks_per_step, pages_per_block=ppb),
        grid_spec=grid_spec,
        out_shape=jax.ShapeDtypeStruct((b, h, LANES), jnp.int32),
        compiler_params=_cparams(("parallel", "arbitrary")),
        name="moba_select",
    )(page_table, q, *([cache_k] * n_in))


def _moba_decode_kernel(pg_ref, pos_ref, slope_ref, q_ref, ks_ref, vs_ref, *rest,
                        page, past_len, n_pages):
    k_refs = rest[:n_pages]
    v_refs = rest[n_pages:2 * n_pages]
    o_ref = rest[2 * n_pages]
    m_ref, l_ref, acc_ref = rest[2 * n_pages + 1:]
    bi = pl.program_id(0)
    h = pl.program_id(1)
    hl = h % SUBLANES
    d = MOBA_HEAD_DIM
    scale = d ** -0.5
    q = q_ref[0]
    slope = slope_ref[...]
    ones = _group_ones(d)

    @pl.when(hl == 0)
    def _():
        m_ref[...] = jnp.full_like(m_ref, NEG_INF)
        l_ref[...] = jnp.zeros_like(l_ref)
        acc_ref[...] = jnp.zeros_like(acc_ref)

    def update(ss, vv, live):
        m_prev = m_ref[...]
        m_all = m_prev
        for s in ss:
            m_all = jnp.maximum(m_all, jnp.max(s, axis=0))
        m_new = jnp.where(live, m_all, m_prev)
        alpha = jnp.exp(m_prev - m_new)
        l_new = alpha * l_ref[...]
        acc_new = alpha * acc_ref[...]
        for s, v in zip(ss, vv):
            p = jnp.where(live[None], jnp.exp(s - m_new[None]), 0.0)
            l_new = l_new + jnp.sum(p, axis=0)
            acc_new = acc_new + jnp.sum(p * v, axis=0)
        l_ref[...] = l_new
        acc_ref[...] = acc_new
        m_ref[...] = m_new

    t_idx = lax.broadcasted_iota(jnp.int32, (page, SUBLANES, d), 0)
    ss = []
    for r in range(n_pages):
        kpos = pos_ref[bi, h * n_pages + r] + t_idx
        dist = (past_len - kpos).astype(F32)
        ss.append(_lane_group_sums(k_refs[r][0] * q[None], ones) * scale - slope[None] * dist)
    row = lax.broadcasted_iota(jnp.int32, (SUBLANES, d), 0)
    update(ss, [v_refs[r][0] for r in range(n_pages)], row == hl)

    @pl.when(hl == SUBLANES - 1)
    def _():
        s_self = _lane_group_sums(ks_ref[0] * q, ones) * scale
        update([s_self[None]], [vs_ref[0][None]], row >= 0)
        o_ref[0] = acc_ref[...] / l_ref[...]


def _moba_decode(q, ks, vs, cache_k, cache_v, phys_pages, first_pos, slopes, *, past_len):
    b, h, d = q.shape
    page = cache_k.shape[1]
    n_pages = phys_pages.shape[1] // h

    def cspec(r):
        return pl.BlockSpec((1, page, SUBLANES, d),
                            lambda bi, hh, pg, pos: (pg[bi, hh * n_pages + r], 0, hh // SUBLANES, 0))

    vec = pl.BlockSpec((1, SUBLANES, d), lambda bi, hh, pg, pos: (bi, hh // SUBLANES, 0))
    grid_spec = pltpu.PrefetchScalarGridSpec(
        num_scalar_prefetch=2,
        grid=(b, h),
        in_specs=[pl.BlockSpec((SUBLANES, d), lambda bi, hh, pg, pos: (hh // SUBLANES, 0)),
                  vec, vec, vec]
                 + [cspec(r) for r in range(n_pages)]
                 + [cspec(r) for r in range(n_pages)],
        out_specs=vec,
        scratch_shapes=[pltpu.VMEM((SUBLANES, d), F32),
                        pltpu.VMEM((SUBLANES, d), F32),
                        pltpu.VMEM((SUBLANES, d), F32)],
    )
    return pl.pallas_call(
        functools.partial(_moba_decode_kernel, page=page, past_len=past_len, n_pages=n_pages),
        grid_spec=grid_spec,
        out_shape=jax.ShapeDtypeStruct((b, h, d), F32),
        compiler_params=_cparams(("parallel", "arbitrary")),
        name="moba_decode",
    )(phys_pages, first_pos, slopes, q, ks, vs,
      *([cache_k] * n_pages), *([cache_v] * n_pages))


def _alibi_slopes(n_heads):
    return jnp.asarray([2.0 ** (-8.0 * (h + 1) / n_heads) for h in range(n_heads)], F32)


def _lambda_init_value(layer_idx):
    return 0.8 - 0.6 * math.exp(-0.3 * layer_idx)


def kernel(x_prompt, x_sample, cache_k_l0, cache_v_l0, cache_k_l1, cache_v_l1, page_table,
           w_qkv, w_o, w_gate_up, w_down, g_mix_pre, g_mix_post, g_ffn_pre, g_ffn_post,
           diff_lambda, diff_subln):
    bp, sp, dm = x_prompt.shape
    bs, ds, _ = x_sample.shape
    assert ds == 1
    depth = w_qkv.shape[0]
    page = cache_k_l0.shape[1]
    past_len = page_table.shape[1] * page
    diff_heads = cache_k_l0.shape[2]
    moba_heads = cache_k_l1.shape[2]

    hp = x_prompt.reshape(bp * sp, dm)
    hs = x_sample.reshape(bs * ds, dm)
    wqkv_b = w_qkv.astype(BF16)
    wo_b = w_o.astype(BF16)
    wgu_b = w_gate_up.astype(BF16)
    wdn_b = w_down.astype(BF16)

    rows = []
    for i in range(depth):
        g_pre = g_mix_pre[i].reshape(1, dm)
        head_dim = DIFF_HEAD_DIM if i % 2 == 0 else MOBA_HEAD_DIM
        qp, kp, vp, kpb, vpt = _norm_qkv(hp, g_pre, wqkv_b, i, tm=1024, tn=512, q_dtype=BF16,
                                         tile=ATTN_TILE, q_scale=head_dim ** -0.5 * LOG2E)
        qs, ks, vs = _norm_qkv(hs, g_pre, w_qkv, i, tm=bs * ds, tn=512, q_dtype=F32)
        if i % 2 == 0:
            jd = i // 2
            lam_init = _lambda_init_value(i)
            hw = 2 * DIFF_HEAD_DIM
            sl = _alibi_slopes(diff_heads)
            slopes_p = jnp.broadcast_to(sl[:, None, None], (diff_heads, SUBLANES, LANES))
            slopes_s = jnp.broadcast_to(sl[:, None], (diff_heads, hw))
            subln = diff_subln[jd].reshape(1, hw)
            subln_col = jnp.broadcast_to(diff_subln[jd][:, None], (hw, ATTN_TILE))
            op = _diff_prompt(qp, kpb, vpt, slopes_p, diff_lambda[jd], subln_col,
                              batch=bp, seq=sp, heads=diff_heads, lam_init=lam_init,
                              tile=ATTN_TILE, heads_per_step=4)
            os_ = _diff_decode(qs.reshape(bs, diff_heads, hw), ks.reshape(bs, diff_heads, hw),
                               vs.reshape(bs, diff_heads, hw), cache_k_l0, cache_v_l0,
                               page_table, slopes_s, diff_lambda[jd], subln,
                               lam_init=lam_init, pages_per_step=8)
            os_ = os_.reshape(bs * ds, dm)
            rows.append((kp.reshape(bp, sp, diff_heads, hw), vp.reshape(bp, sp, diff_heads, hw),
                         ks.reshape(bs, ds, diff_heads, hw), vs.reshape(bs, ds, diff_heads, hw)))
        else:
            d = MOBA_HEAD_DIM
            sl = _alibi_slopes(moba_heads)
            slopes_p = jnp.broadcast_to(sl[:, None, None], (moba_heads, SUBLANES, LANES))
            slopes_s = jnp.broadcast_to(sl[:, None], (moba_heads, d))
            op = _moba_prompt(qp, kpb, vpt, slopes_p, batch=bp, seq=sp, heads=moba_heads,
                              heads_per_step=8)
            q3 = qs.reshape(bs, moba_heads, d)
            k3 = ks.reshape(bs, moba_heads, d)
            v3 = vs.reshape(bs, moba_heads, d)
            sel = _moba_select(q3, cache_k_l1, page_table,
                               blocks_per_step=2)[:, :, :MOBA_TOPK]
            ppb = MOBA_BLOCK // page
            logical = (sel[..., None] * ppb + jnp.arange(ppb, dtype=jnp.int32)).reshape(bs, -1)
            phys = jnp.take_along_axis(page_table, logical, axis=1)
            os_ = _moba_decode(q3, k3, v3, cache_k_l1, cache_v_l1, phys, logical * page,
                               slopes_s, past_len=past_len)
            os_ = os_.reshape(bs * ds, dm)
            rows.append((kp.reshape(bp, sp, moba_heads, d), vp.reshape(bp, sp, moba_heads, d),
                         ks.reshape(bs, ds, moba_heads, d), vs.reshape(bs, ds, moba_heads, d)))

        g_post = g_mix_post[i].reshape(1, dm)
        feeds_topk = any(j % 2 == 1 for j in range(i + 1, depth))
        wo_s, wgu_s, wdn_s = (w_o, w_gate_up, w_down) if feeds_topk else (wo_b, wgu_b, wdn_b)
        hp = _mm_norm_res(op, wo_b, i, g_post, hp, tm=512)
        hs = _mm_norm_res(os_, wo_s, i, g_post, hs, tm=bs * ds, tk=512)
        g_fpre = g_ffn_pre[i].reshape(1, dm)
        g_fpost = g_ffn_post[i].reshape(1, dm)
        fp = _norm_swiglu(hp, g_fpre, wgu_b, i, tm=1024, tn=512)
        hp = _mm_norm_res(fp, wdn_b, i, g_fpost, hp, tm=512)
        fs = _norm_swiglu(hs, g_fpre, wgu_s, i, tm=bs * ds, tn=512)
        hs = _mm_norm_res(fs, wdn_s, i, g_fpost, hs, tm=bs * ds, tk=512)

    (k0p, v0p, k0s, v0s), (k1p, v1p, k1s, v1s) = rows
    return (hp.reshape(bp, sp, dm), hs.reshape(bs, ds, dm),
            k0p, v0p, k0s, v0s, k1p, v1p, k1s, v1s)
```

```python
import functools
import math

import jax
import jax.numpy as jnp
from jax import lax
from jax.experimental import pallas as pl
from jax.experimental.pallas import tpu as pltpu

F32 = jnp.float32
BF16 = jnp.bfloat16

NORM_EPS = 1e-6
NEG_INF = -1e30
LOG2E = math.log2(math.e)
DIFF_HEAD_DIM = 128
MOBA_HEAD_DIM = 128
MOBA_BLOCK = 256
MOBA_TOPK = 3
ATTN_TILE = MOBA_BLOCK
KV_GROUP = 2

LANES = 128
SUBLANES = 8
VMEM_LIMIT = 56 * 1024 * 1024


def _cparams(sem):
    return pltpu.CompilerParams(dimension_semantics=sem, vmem_limit_bytes=VMEM_LIMIT)


def _nt_dot(a, b):
    return lax.dot_general(a, b, (((1,), (1,)), ((), ())), preferred_element_type=F32)


def _mm(a, w):
    precision = lax.Precision.HIGHEST if w.dtype == F32 else None
    return jnp.dot(a.astype(w.dtype), w, preferred_element_type=F32, precision=precision)


def _rms_scale(x):
    return lax.rsqrt(jnp.mean(x * x, axis=-1, keepdims=True) + NORM_EPS)


def _norm_qkv_kernel(x_ref, g_ref, wq_ref, wk_ref, wv_ref, q_ref, k_ref, v_ref, *rest,
                     tile, q_scale):
    xn_ref = rest[-1]

    @pl.when(pl.program_id(1) == 0)
    def _():
        x = x_ref[...]
        xn_ref[...] = ((x * _rms_scale(x)) * g_ref[...]).astype(xn_ref.dtype)

    xn = xn_ref[...]
    q = _mm(xn, wq_ref[...])
    k = _mm(xn, wk_ref[...])
    v = _mm(xn, wv_ref[...])
    q_ref[...] = (q if q_scale is None else q * q_scale).astype(q_ref.dtype)
    k_ref[...] = k
    v_ref[...] = v
    if len(rest) == 3:
        kb_ref, vt_ref = rest[:2]
        kb_ref[...] = k.astype(BF16)
        for u in range(vt_ref.shape[0]):
            vt_ref[u] = v[u * tile:(u + 1) * tile, :].T.astype(BF16)


def _norm_qkv(x, g, w, layer, *, tm, tn, q_dtype, tile=None, q_scale=None):
    m, d = x.shape
    nj = d // tn
    wspec = lambda off: pl.BlockSpec((None, d, tn), lambda i, j: (layer, 0, j + off))
    ospec = pl.BlockSpec((tm, tn), lambda i, j: (i, j))
    out_specs = [ospec] * 3
    out_shape = [jax.ShapeDtypeStruct((m, d), q_dtype),
                 jax.ShapeDtypeStruct((m, d), F32),
                 jax.ShapeDtypeStruct((m, d), F32)]
    if tile is not None:
        out_specs += [ospec, pl.BlockSpec((tm // tile, tn, tile), lambda i, j: (i, j, 0))]
        out_shape += [jax.ShapeDtypeStruct((m, d), BF16),
                      jax.ShapeDtypeStruct((m // tile, d, tile), BF16)]
    return pl.pallas_call(
        functools.partial(_norm_qkv_kernel, tile=tile, q_scale=q_scale),
        grid=(m // tm, nj),
        in_specs=[pl.BlockSpec((tm, d), lambda i, j: (i, 0)),
                  pl.BlockSpec((1, d), lambda i, j: (0, 0)),
                  wspec(0), wspec(nj), wspec(2 * nj)],
        out_specs=out_specs,
        out_shape=out_shape,
        scratch_shapes=[pltpu.VMEM((tm, d), w.dtype)],
        compiler_params=_cparams(("parallel", "arbitrary")),
        name="norm_qkv",
    )(x, g, w, w, w)


def _norm_swiglu_kernel(x_ref, g_ref, wg_ref, wu_ref, h_ref, xn_ref):
    @pl.when(pl.program_id(1) == 0)
    def _():
        x = x_ref[...]
        xn_ref[...] = ((x * _rms_scale(x)) * g_ref[...]).astype(xn_ref.dtype)

    xn = xn_ref[...]
    gate = _mm(xn, wg_ref[...])
    up = _mm(xn, wu_ref[...])
    h_ref[...] = ((gate * (1.0 / (1.0 + jnp.exp(-gate)))) * up).astype(h_ref.dtype)


def _norm_swiglu(x, g, w, layer, *, tm, tn):
    m, d = x.shape
    f = w.shape[2] // 2
    nj = f // tn
    return pl.pallas_call(
        _norm_swiglu_kernel,
        grid=(m // tm, nj),
        in_specs=[pl.BlockSpec((tm, d), lambda i, j: (i, 0)),
                  pl.BlockSpec((1, d), lambda i, j: (0, 0)),
                  pl.BlockSpec((None, d, tn), lambda i, j: (layer, 0, j)),
                  pl.BlockSpec((None, d, tn), lambda i, j: (layer, 0, j + nj))],
        out_specs=pl.BlockSpec((tm, tn), lambda i, j: (i, j)),
        out_shape=jax.ShapeDtypeStruct((m, f), w.dtype),
        scratch_shapes=[pltpu.VMEM((tm, d), w.dtype)],
        compiler_params=_cparams(("parallel", "arbitrary")),
        name="norm_swiglu",
    )(x, g, w, w)


def _mm_norm_res_kernel(a_ref, w_ref, g_ref, r_ref, o_ref, *acc):
    def finish(y):
        o_ref[...] = r_ref[...] + (y * _rms_scale(y)) * g_ref[...]

    if not acc:
        finish(_mm(a_ref[...], w_ref[...]))
        return

    acc_ref, = acc
    kk = pl.program_id(1)

    @pl.when(kk == 0)
    def _():
        acc_ref[...] = jnp.zeros_like(acc_ref)

    acc_ref[...] += _mm(a_ref[...], w_ref[...])

    @pl.when(kk == pl.num_programs(1) - 1)
    def _():
        finish(acc_ref[...])


def _mm_norm_res(a, w, layer, g, res, *, tm, tk=None):
    m, kdim = a.shape
    n = w.shape[2]
    if tk is None:
        tk = kdim
        w_spec = pl.BlockSpec((None, kdim, n), lambda i, k: (layer, 0, 0),
                              pipeline_mode=pl.Buffered(1))
        scratch = []
    else:
        w_spec = pl.BlockSpec((None, tk, n), lambda i, k: (layer, k, 0))
        scratch = [pltpu.VMEM((tm, n), F32)]
    return pl.pallas_call(
        _mm_norm_res_kernel,
        grid=(m // tm, kdim // tk),
        in_specs=[pl.BlockSpec((tm, tk), lambda i, k: (i, k)),
                  w_spec,
                  pl.BlockSpec((1, n), lambda i, k: (0, 0)),
                  pl.BlockSpec((tm, n), lambda i, k: (i, 0))],
        out_specs=pl.BlockSpec((tm, n), lambda i, k: (i, 0)),
        out_shape=jax.ShapeDtypeStruct((m, n), F32),
        scratch_shapes=scratch,
        compiler_params=_cparams(("parallel", "arbitrary")),
        name="mm_norm_res",
    )(a, w, g, res)


def _diff_lambda(lp, lam_init):
    a = jnp.sum(lp[0:1] * lp[1:2], axis=-1, keepdims=True)
    b = jnp.sum(lp[2:3] * lp[3:4], axis=-1, keepdims=True)
    return jnp.exp(a) - jnp.exp(b) + lam_init


def _alibi_operands(slope2, tile):
    assert tile <= 256
    lane = lax.broadcasted_iota(jnp.int32, (tile, LANES), 1)
    row = lax.broadcasted_iota(jnp.int32, (tile, LANES), 0).astype(F32)
    k_side = jnp.where(lane < 3, row, 0.0).astype(BF16)
    hi = slope2.astype(BF16).astype(F32)
    mid = (slope2 - hi).astype(BF16).astype(F32)
    lo = (slope2 - hi - mid).astype(BF16).astype(F32)
    q_side = jnp.where(lane == 0, hi, jnp.where(lane == 1, mid, jnp.where(lane == 2, lo, 0.0)))
    return k_side, q_side.astype(BF16)


def _diff_prompt_kernel(slope_ref, lam_ref, subln_ref, q_ref, k_ref, vt_ref, o_ref,
                        acc_ref, sa_ref, sb_ref, *, tile, lam_init):
    i = pl.program_id(2)
    dk = DIFF_HEAD_DIM
    hw = 2 * dk
    gk = KV_GROUP * tile
    last_group = k_ref.shape[0] // gk - 1
    n_heads = slope_ref.shape[0]
    chains = [(h, c) for h in range(n_heads) for c in range(2)]
    krow = lax.broadcasted_iota(jnp.int32, (tile, tile), 0)
    qcol = lax.broadcasted_iota(jnp.int32, (tile, tile), 1)
    slope2 = [slope_ref[h][0:1, 0:1] * LOG2E for h in range(n_heads)]
    sides = [_alibi_operands(slope2[h], tile) for h in range(n_heads)]
    k_side = sides[0][0]
    k_side_g = jnp.concatenate([k_side] * KV_GROUP, axis=0)
    q_aug = {(h, c): jnp.concatenate([q_ref[:, h * hw + c * dk:h * hw + (c + 1) * dk],
                                      sides[h][1]], axis=-1) for h, c in chains}
    acc_ref[...] = jnp.zeros_like(acc_ref)

    def scores(k, side, h, c):
        return _nt_dot(jnp.concatenate([k[:, h * hw + c * dk:h * hw + (c + 1) * dk], side],
                                       axis=-1), q_aug[h, c])

    def produce(s_ref, g):
        k = k_ref[pl.ds(pl.multiple_of(g * gk, gk), gk), :]
        for n, (h, c) in enumerate(chains):
            s = scores(k, k_side_g, h, c)
            for u in range(KV_GROUP):
                s_ref[n, u] = s[u * tile:(u + 1) * tile]

    def update(tiles, get, shifts, state):
        out = []
        for n, (h, c) in enumerate(chains):
            m_prev, l_prev = state[2 * n], state[2 * n + 1]
            m_new = m_prev
            for u, shift in enumerate(shifts[h]):
                m_new = jnp.maximum(m_new, jnp.max(get(n, u), axis=0, keepdims=True) + shift)
            alpha = jnp.exp2(m_prev - m_new)
            l_new = alpha * l_prev
            pv = None
            for u, (t, shift) in enumerate(zip(tiles, shifts[h])):
                p = jnp.exp2(get(n, u) - (m_new - shift))
                l_new = l_new + jnp.sum(p, axis=0, keepdims=True)
                part = jnp.dot(vt_ref[t, h * hw:(h + 1) * hw, :], p.astype(BF16),
                               preferred_element_type=F32)
                pv = part if pv is None else pv + part
            acc_ref[n] = alpha * acc_ref[n] + pv
            out += [m_new, l_new]
        return tuple(out)

    def consume(s_ref, g, state):
        tiles = [g * KV_GROUP + u for u in range(KV_GROUP)]
        shifts = [[jnp.where(t < i, -slope2[h] * ((i - t) * tile).astype(F32), NEG_INF)
                   for t in tiles] for h in range(n_heads)]
        return update(tiles, lambda n, u: s_ref[n, u], shifts, state)

    def body(jj, state):
        produce(sb_ref, 2 * jj + 1)
        state = consume(sa_ref, 2 * jj, state)
        produce(sa_ref, jnp.minimum(2 * jj + 2, last_group))
        return consume(sb_ref, 2 * jj + 1, state)

    neg = jnp.full((1, tile), NEG_INF, F32)
    zero = jnp.zeros((1, tile), F32)
    produce(sa_ref, 0)
    n_pairs = (i + 2 * KV_GROUP - 1) // (2 * KV_GROUP)
    state = lax.fori_loop(0, n_pairs, body, (neg, zero) * len(chains))
    k_diag = k_ref[pl.ds(pl.multiple_of(i * tile, tile), tile), :]
    diag = [jnp.where(qcol >= krow, scores(k_diag, k_side, h, c), NEG_INF) for h, c in chains]
    no_shift = [[jnp.zeros((1, 1), F32)]] * n_heads
    state = update([i], lambda n, u: diag[n], no_shift, state)

    lam = _diff_lambda(lam_ref[...], lam_init)
    for h in range(n_heads):
        l0, l1 = state[4 * h + 1], state[4 * h + 3]
        o = acc_ref[2 * h] * (1.0 / l0) - lam * (acc_ref[2 * h + 1] * (1.0 / l1))
        inv_rms = lax.rsqrt(jnp.mean(o * o, axis=0, keepdims=True) + NORM_EPS)
        o = ((o * inv_rms) * subln_ref[...]) * (1.0 - lam_init)
        o_ref[:, h * hw:(h + 1) * hw] = o.T.astype(o_ref.dtype)


def _diff_prompt(q, k, vt, slopes, lam_params, subln_col, *, batch, seq, heads, lam_init, tile,
                 heads_per_step):
    hw = 2 * DIFF_HEAD_DIM
    nq = seq // tile
    hp = heads_per_step
    kern = functools.partial(_diff_prompt_kernel, tile=tile, lam_init=lam_init)
    return pl.pallas_call(
        kern,
        grid=(batch, heads // hp, nq),
        in_specs=[pl.BlockSpec((hp, SUBLANES, LANES), lambda b, h, i: (h, 0, 0)),
                  pl.BlockSpec((4, DIFF_HEAD_DIM), lambda b, h, i: (0, 0)),
                  pl.BlockSpec((hw, tile), lambda b, h, i: (0, 0)),
                  pl.BlockSpec((tile, hp * hw), lambda b, h, i: (b * nq + i, h)),
                  pl.BlockSpec((seq, hp * hw), lambda b, h, i: (b, h)),
                  pl.BlockSpec((nq, hp * hw, tile), lambda b, h, i: (b, h, 0))],
        out_specs=pl.BlockSpec((tile, hp * hw), lambda b, h, i: (b * nq + i, h)),
        out_shape=jax.ShapeDtypeStruct(q.shape, BF16),
        scratch_shapes=[pltpu.VMEM((2 * hp, hw, tile), F32),
                        pltpu.VMEM((2 * hp, KV_GROUP, tile, tile), F32),
                        pltpu.VMEM((2 * hp, KV_GROUP, tile, tile), F32)],
        compiler_params=_cparams(("parallel", "parallel", "arbitrary")),
        name="diff_prompt",
    )(slopes, lam_params, subln_col, q, k, vt)


def _group_ones(width):
    r = lax.broadcasted_iota(jnp.int32, (width, width), 0) // LANES
    c = lax.broadcasted_iota(jnp.int32, (width, width), 1) // LANES
    return (r == c).astype(BF16)


def _lane_group_sums(x, ones):
    shape = x.shape
    x2 = x.reshape((-1, shape[-1])).astype(BF16)
    return jnp.dot(x2, ones, preferred_element_type=F32).reshape(shape)


def _twice(x):
    return jnp.concatenate([x, x], axis=-1)


def _diff_decode_kernel(pt_ref, slope_ref, lam_ref, subln_ref, q_ref, ks_ref, vs_ref, *rest,
                        pages_per_step, page, past_len, lam_init):
    npg = pages_per_step
    k_refs = rest[:npg]
    v_refs = rest[npg:2 * npg]
    o_ref = rest[2 * npg]
    m_ref, l_ref, acc_ref, tb_ref = rest[2 * npg + 1:]
    j = pl.program_id(1)
    dk = DIFF_HEAD_DIM
    q2 = q_ref[0] * (dk ** -0.5 * LOG2E)
    slope2 = slope_ref[...] * LOG2E
    ones = _group_ones(2 * dk)

    @pl.when(j == 0)
    def _():
        m_ref[...] = jnp.full_like(m_ref, NEG_INF)
        l_ref[...] = jnp.zeros_like(l_ref)
        acc_ref[...] = jnp.zeros_like(acc_ref)
        t_idx = lax.broadcasted_iota(jnp.int32, tb_ref.shape, 0)
        tb_ref[...] = slope2[None] * t_idx.astype(F32)

    def update(ss, shifts, vv):
        m_prev = m_ref[...]
        m_new = m_prev
        for s, shift in zip(ss, shifts):
            m_new = jnp.maximum(m_new, jnp.max(s, axis=0) + shift)
        alpha = jnp.exp2(m_prev - m_new)
        l_new = alpha * l_ref[...]
        acc_new = [_twice(alpha[:, c * dk:(c + 1) * dk]) * acc_ref[c] for c in range(2)]
        for s, shift, v in zip(ss, shifts, vv):
            p = jnp.exp2(s - (m_new - shift)[None])
            l_new = l_new + jnp.sum(p, axis=0)
            for c in range(2):
                pc = _twice(p[:, :, c * dk:(c + 1) * dk])
                acc_new[c] = acc_new[c] + jnp.sum(pc * v, axis=0)
        l_ref[...] = l_new
        for c in range(2):
            acc_ref[c] = acc_new[c]
        m_ref[...] = m_new

    ss, shifts = [], []
    for pg in range(npg):
        base = (j * npg + pg) * page
        shifts.append(-slope2 * (past_len - base).astype(F32))
        ss.append(_lane_group_sums(k_refs[pg][0] * q2[None], ones) + tb_ref[...])
    for pg in range(npg):
        update([ss[pg]], [shifts[pg]], [v_refs[pg][0]])

    @pl.when(j == pl.num_programs(1) - 1)
    def _():
        s_self = _lane_group_sums(ks_ref[0] * q2, ones)
        update([s_self[None]], [jnp.zeros_like(slope2)], [vs_ref[0][None]])
        lam = _diff_lambda(lam_ref[...], lam_init)
        l = l_ref[...]
        o = (acc_ref[0] / _twice(l[:, :dk])) - lam * (acc_ref[1] / _twice(l[:, dk:]))
        o = ((o * _rms_scale(o)) * subln_ref[...]) * (1.0 - lam_init)
        o_ref[0] = o


def _diff_decode(q, ks, vs, cache_k, cache_v, page_table, slopes, lam_params, subln, *,
                 lam_init, pages_per_step):
    b, h, w = q.shape
    page = cache_k.shape[1]
    n_pages = page_table.shape[1]
    npg = pages_per_step
    kern = functools.partial(_diff_decode_kernel, pages_per_step=npg, page=page,
                             past_len=n_pages * page, lam_init=lam_init)

    def cache_spec(pg):
        return pl.BlockSpec((1, page, h, w),
                            lambda bi, j, pt: (pt[bi, j * npg + pg], 0, 0, 0))

    vec = pl.BlockSpec((1, h, w), lambda bi, j, pt: (bi, 0, 0))
    grid_spec = pltpu.PrefetchScalarGridSpec(
        num_scalar_prefetch=1,
        grid=(b, n_pages // npg),
        in_specs=[pl.BlockSpec((h, w), lambda bi, j, pt: (0, 0)),
                  pl.BlockSpec((4, DIFF_HEAD_DIM), lambda bi, j, pt: (0, 0)),
                  pl.BlockSpec((1, w), lambda bi, j, pt: (0, 0)),
                  vec, vec, vec]
                 + [cache_spec(pg) for pg in range(npg)]
                 + [cache_spec(pg) for pg in range(npg)],
        out_specs=vec,
        scratch_shapes=[pltpu.VMEM((h, w), F32),
                        pltpu.VMEM((h, w), F32),
                        pltpu.VMEM((2, h, w), F32),
                        pltpu.VMEM((page, h, w), F32)],
    )
    return pl.pallas_call(
        kern,
        grid_spec=grid_spec,
        out_shape=jax.ShapeDtypeStruct((b, h, w), F32),
        compiler_params=_cparams(("parallel", "arbitrary")),
        name="diff_decode",
    )(page_table, slopes, lam_params, subln, q, ks, vs,
      *([cache_k] * npg), *([cache_v] * npg))


def _top_blocks(gate, n_valid, idx, axis):
    valid = idx < n_valid
    g = jnp.where(valid, gate, NEG_INF)
    chosen = jnp.zeros(gate.shape, F32)
    past_end = float(gate.shape[axis])
    for _ in range(MOBA_TOPK):
        mx = jnp.max(g, axis=axis, keepdims=True)
        first = jnp.min(jnp.where(g == mx, idx, past_end), axis=axis, keepdims=True)
        pick = idx == first
        chosen = jnp.where(pick & valid, 1.0, chosen)
        g = jnp.where(pick, -jnp.inf, g)
    return chosen


def _moba_prompt_kernel(slope_ref, q_ref, k_ref, vt_ref, o_ref, acc_ref, gb_ref, km_ref,
                        sa_ref, sb_ref):
    i = pl.program_id(2)
    d = MOBA_HEAD_DIM
    blk = MOBA_BLOCK
    gk = KV_GROUP * blk
    last_group = k_ref.shape[0] // gk - 1
    n_heads = slope_ref.shape[0]
    heads = range(n_heads)
    krow = lax.broadcasted_iota(jnp.int32, (blk, blk), 0)
    qcol = lax.broadcasted_iota(jnp.int32, (blk, blk), 1)
    slope2 = [slope_ref[h][0:1, 0:1] * LOG2E for h in heads]
    sides = [_alibi_operands(slope2[h], blk) for h in heads]
    k_side = sides[0][0]
    k_side_g = jnp.concatenate([k_side] * KV_GROUP, axis=0)
    q = [q_ref[:, h * d:(h + 1) * d] for h in heads]
    q_aug = [jnp.concatenate([q[h], sides[h][1]], axis=-1) for h in heads]

    def scores(k, side, h):
        return _nt_dot(jnp.concatenate([k[:, h * d:(h + 1) * d], side], axis=-1), q_aug[h])

    @pl.when(i == 0)
    def _():
        for h in heads:
            k_all = k_ref[:, h * d:(h + 1) * d].astype(F32)
            km_ref[h] = jnp.mean(k_all.reshape(-1, blk, d), axis=1)

    for h in heads:
        km = km_ref[h]
        km_hi = km.astype(BF16)
        km_lo = (km - km_hi.astype(F32)).astype(BF16)
        gate = _nt_dot(km_hi, q[h]) + _nt_dot(km_lo, q[h])
        nrow = lax.broadcasted_iota(jnp.int32, gate.shape, 0).astype(F32)
        chosen = _top_blocks(gate, i.astype(F32), nrow, 0)
        gb_ref[h] = jnp.where(chosen > 0.5, 0.0, NEG_INF)

    k_own = k_ref[pl.ds(pl.multiple_of(i * blk, blk), blk), :]
    state = []
    for h in heads:
        s = jnp.where(qcol >= krow, scores(k_own, k_side, h), NEG_INF)
        m0 = jnp.max(s, axis=0, keepdims=True)
        p = jnp.exp2(s - m0)
        acc_ref[h] = jnp.dot(vt_ref[i, h * d:(h + 1) * d, :], p.astype(BF16),
                             preferred_element_type=F32)
        state += [m0, jnp.sum(p, axis=0, keepdims=True)]

    def produce(s_ref, g):
        k = k_ref[pl.ds(pl.multiple_of(g * gk, gk), gk), :]
        for h in heads:
            s = scores(k, k_side_g, h)
            for u in range(KV_GROUP):
                s_ref[h, u] = s[u * blk:(u + 1) * blk]

    def consume(s_ref, g, state):
        blocks = [g * KV_GROUP + u for u in range(KV_GROUP)]
        out = []
        for h in heads:
            m_prev, l_prev = state[2 * h], state[2 * h + 1]
            shifts = [gb_ref[h, pl.ds(n, 1), :] - slope2[h] * ((i - n) * blk).astype(F32)
                      for n in blocks]
            m_new = m_prev
            for u, shift in enumerate(shifts):
                m_new = jnp.maximum(m_new, jnp.max(s_ref[h, u], axis=0, keepdims=True) + shift)
            alpha = jnp.exp2(m_prev - m_new)
            l_new = alpha * l_prev
            pv = None
            for u, (n, shift) in enumerate(zip(blocks, shifts)):
                p = jnp.exp2(s_ref[h, u] - (m_new - shift))
                l_new = l_new + jnp.sum(p, axis=0, keepdims=True)
                part = jnp.dot(vt_ref[n, h * d:(h + 1) * d, :], p.astype(BF16),
                               preferred_element_type=F32)
                pv = part if pv is None else pv + part
            acc_ref[h] = alpha * acc_ref[h] + pv
            out += [m_new, l_new]
        return tuple(out)

    def body(jj, state):
        produce(sb_ref, 2 * jj + 1)
        state = consume(sa_ref, 2 * jj, state)
        produce(sa_ref, jnp.minimum(2 * jj + 2, last_group))
        return consume(sb_ref, 2 * jj + 1, state)

    produce(sa_ref, 0)
    n_pairs = (i + 2 * KV_GROUP - 1) // (2 * KV_GROUP)
    state = lax.fori_loop(0, n_pairs, body, tuple(state))
    for h in heads:
        o_ref[:, h * d:(h + 1) * d] = (acc_ref[h] * (1.0 / state[2 * h + 1])).T.astype(o_ref.dtype)


def _moba_prompt(q, k, vt, slopes, *, batch, seq, heads, heads_per_step):
    d = MOBA_HEAD_DIM
    blk = MOBA_BLOCK
    nq = seq // blk
    hp = heads_per_step
    return pl.pallas_call(
        _moba_prompt_kernel,
        grid=(batch, heads // hp, nq),
        in_specs=[pl.BlockSpec((hp, SUBLANES, LANES), lambda b, h, i: (h, 0, 0)),
                  pl.BlockSpec((blk, hp * d), lambda b, h, i: (b * nq + i, h)),
                  pl.BlockSpec((seq, hp * d), lambda b, h, i: (b, h)),
                  pl.BlockSpec((nq, hp * d, blk), lambda b, h, i: (b, h, 0))],
        out_specs=pl.BlockSpec((blk, hp * d), lambda b, h, i: (b * nq + i, h)),
        out_shape=jax.ShapeDtypeStruct(q.shape, BF16),
        scratch_shapes=[pltpu.VMEM((hp, d, blk), F32),
                        pltpu.VMEM((hp, nq, blk), F32),
                        pltpu.VMEM((hp, nq, d), F32),
                        pltpu.VMEM((hp, KV_GROUP, blk, blk), F32),
                        pltpu.VMEM((hp, KV_GROUP, blk, blk), F32)],
        compiler_params=_cparams(("parallel", "parallel", "arbitrary")),
        name="moba_prompt",
    )(slopes, q, k, vt)


def _moba_select_kernel(pt_ref, q_ref, *rest, n_blocks, blocks_per_step, pages_per_block):
    n_in = blocks_per_step * pages_per_block
    k_refs = rest[:n_in]
    sel_ref, gate_ref = rest[n_in:]
    step = pl.program_id(1)
    lane = lax.broadcasted_iota(jnp.int32, gate_ref.shape, 1)

    @pl.when(step == 0)
    def _():
        gate_ref[...] = jnp.full_like(gate_ref, NEG_INF)

    gates = gate_ref[...]
    for u in range(blocks_per_step):
        ksum = jnp.sum(k_refs[u * pages_per_block][0], axis=0)
        for pg in range(1, pages_per_block):
            ksum = ksum + jnp.sum(k_refs[u * pages_per_block + pg][0], axis=0)
        g = jnp.sum(q_ref[0] * (ksum * (1.0 / MOBA_BLOCK)), axis=-1, keepdims=True)
        gates = jnp.where(lane == step * blocks_per_step + u, g, gates)
    gate_ref[...] = gates

    @pl.when(step == pl.num_programs(1) - 1)
    def _():
        lane_f = lane.astype(F32)
        cand = jnp.where(lane < n_blocks, gate_ref[...], NEG_INF)
        sel = jnp.zeros(cand.shape, F32)
        for r in range(MOBA_TOPK):
            mx = jnp.max(cand, axis=-1, keepdims=True)
            first = jnp.min(jnp.where(cand == mx, lane_f, float(LANES)), axis=-1, keepdims=True)
            sel = jnp.where(lane == r, first, sel)
            cand = jnp.where(lane_f == first, -jnp.inf, cand)
        sel_ref[0] = sel.astype(jnp.int32)


def _moba_select(q, cache_k, page_table, *, blocks_per_step):
    b, h, d = q.shape
    page = cache_k.shape[1]
    ppb = MOBA_BLOCK // page
    n_blocks = page_table.shape[1] // ppb
    assert n_blocks <= LANES and n_blocks % blocks_per_step == 0
    n_in = blocks_per_step * ppb

    def kspec(u):
        return pl.BlockSpec((1, page, h, d), lambda bi, n, pt: (pt[bi, n * n_in + u], 0, 0, 0))

    grid_spec = pltpu.PrefetchScalarGridSpec(
        num_scalar_prefetch=1,
        grid=(b, n_blocks // blocks_per_step),
        in_specs=[pl.BlockSpec((1, h, d), lambda bi, n, pt: (bi, 0, 0))]
                 + [kspec(u) for u in range(n_in)],
        out_specs=pl.BlockSpec((1, h, LANES), lambda bi, n, pt: (bi, 0, 0)),
        scratch_shapes=[pltpu.VMEM((h, LANES), F32)],
    )
    return pl.pallas_call(
        functools.partial(_moba_select_kernel, n_blocks=n_blocks,
                          blocks_per_step=blocks_per_step, pages_per_block=ppb),
        grid_spec=grid_spec,
        out_shape=jax.ShapeDtypeStruct((b, h, LANES), jnp.int32),
        compiler_params=_cparams(("parallel", "arbitrary")),
        name="moba_select",
    )(page_table, q, *([cache_k] * n_in))


def _moba_decode_kernel(pg_ref, pos_ref, slope_ref, q_ref, ks_ref, vs_ref, *rest,
                        page, past_len, n_pages):
    k_refs = rest[:n_pages]
    v_refs = rest[n_pages:2 * n_pages]
    o_ref = rest[2 * n_pages]
    m_ref, l_ref, acc_ref, tb_ref = rest[2 * n_pages + 1:]
    bi = pl.program_id(0)
    h = pl.program_id(1)
    hl = h % SUBLANES
    d = MOBA_HEAD_DIM
    q2 = q_ref[0] * (d ** -0.5 * LOG2E)
    slope2 = slope_ref[...] * LOG2E
    ones = _group_ones(d)

    @pl.when(hl == 0)
    def _():
        m_ref[...] = jnp.full_like(m_ref, NEG_INF)
        l_ref[...] = jnp.zeros_like(l_ref)
        acc_ref[...] = jnp.zeros_like(acc_ref)
        t_idx = lax.broadcasted_iota(jnp.int32, tb_ref.shape, 0)
        tb_ref[...] = slope2[None] * t_idx.astype(F32)

    def update(ss, shifts, vv, live):
        m_prev = m_ref[...]
        m_all = m_prev
        for s, shift in zip(ss, shifts):
            m_all = jnp.maximum(m_all, jnp.max(s, axis=0) + shift)
        m_new = jnp.where(live, m_all, m_prev)
        alpha = jnp.exp2(m_prev - m_new)
        l_new = alpha * l_ref[...]
        acc_new = alpha * acc_ref[...]
        for s, shift, v in zip(ss, shifts, vv):
            p = jnp.where(live[None], jnp.exp2(s - (m_new - shift)[None]), 0.0)
            l_new = l_new + jnp.sum(p, axis=0)
            acc_new = acc_new + jnp.sum(p * v, axis=0)
        l_ref[...] = l_new
        acc_ref[...] = acc_new
        m_ref[...] = m_new

    ss, shifts = [], []
    for r in range(n_pages):
        base = pos_ref[bi, h * n_pages + r]
        shifts.append(-slope2 * (past_len - base).astype(F32))
        ss.append(_lane_group_sums(k_refs[r][0] * q2[None], ones) + tb_ref[...])
    row = lax.broadcasted_iota(jnp.int32, (SUBLANES, d), 0)
    update(ss, shifts, [v_refs[r][0] for r in range(n_pages)], row == hl)

    @pl.when(hl == SUBLANES - 1)
    def _():
        s_self = _lane_group_sums(ks_ref[0] * q2, ones)
        update([s_self[None]], [jnp.zeros_like(slope2)], [vs_ref[0][None]], row >= 0)
        o_ref[0] = acc_ref[...] / l_ref[...]


def _moba_decode(q, ks, vs, cache_k, cache_v, phys_pages, first_pos, slopes, *, past_len):
    b, h, d = q.shape
    page = cache_k.shape[1]
    n_pages = phys_pages.shape[1] // h

    def cspec(r):
        return pl.BlockSpec((1, page, SUBLANES, d),
                            lambda bi, hh, pg, pos: (pg[bi, hh * n_pages + r], 0, hh // SUBLANES, 0))

    vec = pl.BlockSpec((1, SUBLANES, d), lambda bi, hh, pg, pos: (bi, hh // SUBLANES, 0))
    grid_spec = pltpu.PrefetchScalarGridSpec(
        num_scalar_prefetch=2,
        grid=(b, h),
        in_specs=[pl.BlockSpec((SUBLANES, d), lambda bi, hh, pg, pos: (hh // SUBLANES, 0)),
                  vec, vec, vec]
                 + [cspec(r) for r in range(n_pages)]
                 + [cspec(r) for r in range(n_pages)],
        out_specs=vec,
        scratch_shapes=[pltpu.VMEM((SUBLANES, d), F32),
                        pltpu.VMEM((SUBLANES, d), F32),
                        pltpu.VMEM((SUBLANES, d), F32),
                        pltpu.VMEM((page, SUBLANES, d), F32)],
    )
    return pl.pallas_call(
        functools.partial(_moba_decode_kernel, page=page, past_len=past_len, n_pages=n_pages),
        grid_spec=grid_spec,
        out_shape=jax.ShapeDtypeStruct((b, h, d), F32),
        compiler_params=_cparams(("parallel", "arbitrary")),
        name="moba_decode",
    )(phys_pages, first_pos, slopes, q, ks, vs,
      *([cache_k] * n_pages), *([cache_v] * n_pages))


def _alibi_slopes(n_heads):
    return jnp.asarray([2.0 ** (-8.0 * (h + 1) / n_heads) for h in range(n_heads)], F32)


def _lambda_init_value(layer_idx):
    return 0.8 - 0.6 * math.exp(-0.3 * layer_idx)


def kernel(x_prompt, x_sample, cache_k_l0, cache_v_l0, cache_k_l1, cache_v_l1, page_table,
           w_qkv, w_o, w_gate_up, w_down, g_mix_pre, g_mix_post, g_ffn_pre, g_ffn_post,
           diff_lambda, diff_subln):
    bp, sp, dm = x_prompt.shape
    bs, ds, _ = x_sample.shape
    assert ds == 1
    depth = w_qkv.shape[0]
    page = cache_k_l0.shape[1]
    past_len = page_table.shape[1] * page
    diff_heads = cache_k_l0.shape[2]
    moba_heads = cache_k_l1.shape[2]

    hp = x_prompt.reshape(bp * sp, dm)
    hs = x_sample.reshape(bs * ds, dm)
    wqkv_b = w_qkv.astype(BF16)
    wo_b = w_o.astype(BF16)
    wgu_b = w_gate_up.astype(BF16)
    wdn_b = w_down.astype(BF16)

    rows = []
    for i in range(depth):
        g_pre = g_mix_pre[i].reshape(1, dm)
        head_dim = DIFF_HEAD_DIM if i % 2 == 0 else MOBA_HEAD_DIM
        qp, kp, vp, kpb, vpt = _norm_qkv(hp, g_pre, wqkv_b, i, tm=1024, tn=512, q_dtype=BF16,
                                         tile=ATTN_TILE, q_scale=head_dim ** -0.5 * LOG2E)
        qs, ks, vs = _norm_qkv(hs, g_pre, w_qkv, i, tm=bs * ds, tn=512, q_dtype=F32)
        if i % 2 == 0:
            jd = i // 2
            lam_init = _lambda_init_value(i)
            hw = 2 * DIFF_HEAD_DIM
            sl = _alibi_slopes(diff_heads)
            slopes_p = jnp.broadcast_to(sl[:, None, None], (diff_heads, SUBLANES, LANES))
            slopes_s = jnp.broadcast_to(sl[:, None], (diff_heads, hw))
            subln = diff_subln[jd].reshape(1, hw)
            subln_col = jnp.broadcast_to(diff_subln[jd][:, None], (hw, ATTN_TILE))
            op = _diff_prompt(qp, kpb, vpt, slopes_p, diff_lambda[jd], subln_col,
                              batch=bp, seq=sp, heads=diff_heads, lam_init=lam_init,
                              tile=ATTN_TILE, heads_per_step=4)
            os_ = _diff_decode(qs.reshape(bs, diff_heads, hw), ks.reshape(bs, diff_heads, hw),
                               vs.reshape(bs, diff_heads, hw), cache_k_l0, cache_v_l0,
                               page_table, slopes_s, diff_lambda[jd], subln,
                               lam_init=lam_init, pages_per_step=8)
            os_ = os_.reshape(bs * ds, dm)
            rows.append((kp.reshape(bp, sp, diff_heads, hw), vp.reshape(bp, sp, diff_heads, hw),
                         ks.reshape(bs, ds, diff_heads, hw), vs.reshape(bs, ds, diff_heads, hw)))
        else:
            d = MOBA_HEAD_DIM
            sl = _alibi_slopes(moba_heads)
            slopes_p = jnp.broadcast_to(sl[:, None, None], (moba_heads, SUBLANES, LANES))
            slopes_s = jnp.broadcast_to(sl[:, None], (moba_heads, d))
            op = _moba_prompt(qp, kpb, vpt, slopes_p, batch=bp, seq=sp, heads=moba_heads,
                              heads_per_step=8)
            q3 = qs.reshape(bs, moba_heads, d)
            k3 = ks.reshape(bs, moba_heads, d)
            v3 = vs.reshape(bs, moba_heads, d)
            sel = _moba_select(q3, cache_k_l1, page_table,
                               blocks_per_step=4)[:, :, :MOBA_TOPK]
            ppb = MOBA_BLOCK // page
            logical = (sel[..., None] * ppb + jnp.arange(ppb, dtype=jnp.int32)).reshape(bs, -1)
            phys = jnp.take_along_axis(page_table, logical, axis=1)
            os_ = _moba_decode(q3, k3, v3, cache_k_l1, cache_v_l1, phys, logical * page,
                               slopes_s, past_len=past_len)
            os_ = os_.reshape(bs * ds, dm)
            rows.append((kp.reshape(bp, sp, moba_heads, d), vp.reshape(bp, sp, moba_heads, d),
                         ks.reshape(bs, ds, moba_heads, d), vs.reshape(bs, ds, moba_heads, d)))

        g_post = g_mix_post[i].reshape(1, dm)
        feeds_topk = any(j % 2 == 1 for j in range(i + 1, depth))
        wo_s, wgu_s, wdn_s = (w_o, w_gate_up, w_down) if feeds_topk else (wo_b, wgu_b, wdn_b)
        hp = _mm_norm_res(op, wo_b, i, g_post, hp, tm=512)
        hs = _mm_norm_res(os_, wo_s, i, g_post, hs, tm=bs * ds, tk=512)
        g_fpre = g_ffn_pre[i].reshape(1, dm)
        g_fpost = g_ffn_post[i].reshape(1, dm)
        fp = _norm_swiglu(hp, g_fpre, wgu_b, i, tm=1024, tn=512)
        hp = _mm_norm_res(fp, wdn_b, i, g_fpost, hp, tm=512)
        fs = _norm_swiglu(hs, g_fpre, wgu_s, i, tm=bs * ds, tn=512)
        hs = _mm_norm_res(fs, wdn_s, i, g_fpost, hs, tm=bs * ds, tk=512)

    (k0p, v0p, k0s, v0s), (k1p, v1p, k1s, v1s) = rows
    return (hp.reshape(bp, sp, dm), hs.reshape(bs, ds, dm),
            k0p, v0p, k0s, v0s, k1p, v1p, k1s, v1s)
```
